```python
import math
import jax, jax.numpy as jnp
from jax import lax
import numpy as np

D_MODEL = 4096
BATCH = 4
SEQ = 2048
DEPTH = 2

HEAD_DIM = 128
CONV_W = 4
GDN_HEADS = D_MODEL // 256
GDN_WIDTH = GDN_HEADS * HEAD_DIM
GDN_CHUNK = 64
RG_WIDTH = D_MODEL // 2
RG_BLOCK = 128
RG_BLOCKS = RG_WIDTH // RG_BLOCK
LRU_C = 8.0
SB_HEADS = D_MODEL // 256
SB_WIDTH = SB_HEADS * HEAD_DIM
SB_BLOCK = 128
S5_WIDTH = D_MODEL // 4
S5_GROUP = 16
S5_GROUPS = S5_WIDTH // S5_GROUP
S5_STATE = 64
PEER_HEADS = 8
PEER_NKEYS = 128
PEER_EXPERTS = PEER_NKEYS * PEER_NKEYS
PEER_QDIM = 256
PEER_HALF = PEER_QDIM // 2
PEER_TOPK = 16
PEER_TOKEN_BLOCK = 64
N_EVEN = (DEPTH + 1) // 2
N_ODD = DEPTH // 2
DN_ALPHA = (2.0 * DEPTH) ** 0.25
DN_BETA = (8.0 * DEPTH) ** -0.25
LN_EPS = 1e-5
RMS_EPS = 1e-6
EVEN_IN = 4 * GDN_WIDTH + 2 * GDN_HEADS + 2 * RG_WIDTH
EVEN_MIX = GDN_WIDTH + RG_WIDTH
ODD_IN = 3 * SB_WIDTH + S5_WIDTH
ODD_MIX = SB_WIDTH + S5_WIDTH

kernel_name = 'hybrid_gdn_rglru_stickbreak_s5_peer'


def _split(t, sizes):
    idx = [int(i) for i in np.cumsum(sizes)[:-1]]
    return jnp.split(t, idx, axis=-1)


def layer_norm(x, g, b):
    xf = x.astype(jnp.float32)
    mu = jnp.mean(xf, -1, keepdims=True)
    var = jnp.mean(jnp.square(xf - mu), -1, keepdims=True)
    return (xf - mu) * lax.rsqrt(var + LN_EPS) * g.astype(jnp.float32) + b.astype(jnp.float32)


def l2norm(t):
    return t * lax.rsqrt(jnp.sum(t * t, -1, keepdims=True) + RMS_EPS)


def causal_dwconv(x, w):
    S_ = x.shape[1]
    xp = jnp.pad(x, ((0, 0), (CONV_W - 1, 0), (0, 0)))
    return sum(w[k] * xp[:, k:k + S_] for k in range(CONV_W))


def _linear_combine(e1, e2):
    a1, b1 = e1
    a2, b2 = e2
    return a1 * a2, a2 * b1 + b2


def _complex_combine(e1, e2):
    a1r, a1i, b1r, b1i = e1
    a2r, a2i, b2r, b2i = e2
    return (a2r * a1r - a2i * a1i, a2r * a1i + a2i * a1r,
            a2r * b1r - a2i * b1i + b2r, a2r * b1i + a2i * b1r + b2i)


def gated_delta_rule(q, k, v, g, beta):
    B_, S_, H, dk = q.shape
    dv = v.shape[-1]
    C = GDN_CHUNK
    n = S_ // C

    def to_chunks(t):
        t = t.reshape((B_, n, C, H) + t.shape[3:])
        return t.transpose((1, 0, 3, 2) + tuple(range(4, t.ndim)))

    q, k, v, g, beta = [to_chunks(t) for t in (q, k, v, g, beta)]
    gc = jnp.cumsum(g, axis=-1)
    idx = jnp.arange(C)
    causal = idx[:, None] >= idx[None, :]
    gamma = jnp.exp(jnp.where(causal, gc[..., :, None] - gc[..., None, :], -jnp.inf))
    kb = k * beta[..., None]
    m = jnp.einsum('nbhid,nbhjd->nbhij', kb, k) * gamma
    m = jnp.where(idx[:, None] > idx[None, :], m, 0.0)
    eye = jnp.eye(C, dtype=m.dtype)
    t_inv = lax.linalg.triangular_solve(m + eye, jnp.broadcast_to(eye, m.shape),
                                        left_side=True, lower=True, unit_diagonal=True)
    u = t_inv @ (v * beta[..., None])
    w = t_inv @ (kb * jnp.exp(gc)[..., None])
    a_qk = jnp.einsum('nbhid,nbhjd->nbhij', q, k) * gamma
    g_last = gc[..., -1]
    k_tail = k * jnp.exp(g_last[..., None] - gc)[..., None]
    q_dec = q * jnp.exp(gc)[..., None]

    def step(state, inp):
        u_i, w_i, qd_i, a_i, kt_i, gl_i = inp
        v_new = u_i - w_i @ state
        o_i = qd_i @ state + a_i @ v_new
        state = state * jnp.exp(gl_i)[..., None, None] + jnp.swapaxes(kt_i, -1, -2) @ v_new
        return state, o_i

    s0 = jnp.zeros((B_, H, dk, dv), jnp.float32)
    _, o = lax.scan(step, s0, (u, w, q_dec, a_qk, k_tail, g_last))
    return o.transpose(1, 0, 3, 2, 4).reshape(B_, S_, H, dv)


def even_mixer(x, w_in, gdn_conv_w, gdn_A_log, gdn_dt_bias, gdn_norm_w,
               rg_conv_w, rg_conv_b, rg_wa, rg_ba, rg_wx, rg_bx, rg_lambda, w_out):
    f32 = jnp.float32
    B_, S_, _ = x.shape
    proj = x @ w_in
    qkv, z, a_raw, b_raw, rg_in, rg_gate = _split(
        proj, [3 * GDN_WIDTH, GDN_WIDTH, GDN_HEADS, GDN_HEADS, RG_WIDTH, RG_WIDTH])
    qkv = jax.nn.silu(causal_dwconv(qkv, gdn_conv_w).astype(f32))
    q, k, v = [t.reshape(B_, S_, GDN_HEADS, HEAD_DIM) for t in jnp.split(qkv, 3, axis=-1)]
    q = l2norm(q) * HEAD_DIM ** -0.5
    k = l2norm(k)
    beta = jax.nn.sigmoid(b_raw.astype(f32))
    g = -jnp.exp(gdn_A_log.astype(f32)) * jax.nn.softplus(a_raw.astype(f32) + gdn_dt_bias.astype(f32))
    o = gated_delta_rule(q, k, v, g, beta)
    o = o * lax.rsqrt(jnp.mean(o * o, -1, keepdims=True) + RMS_EPS) * gdn_norm_w.astype(f32)
    o = o * jax.nn.silu(z.astype(f32).reshape(B_, S_, GDN_HEADS, HEAD_DIM))
    gdn_out = o.reshape(B_, S_, GDN_WIDTH)
    xr = (causal_dwconv(rg_in, rg_conv_w) + rg_conv_b).astype(f32)
    xb = xr.reshape(B_, S_, RG_BLOCKS, RG_BLOCK)
    r = jax.nn.sigmoid(jnp.einsum('bsni,nij->bsnj', xb, rg_wa.astype(f32)).reshape(B_, S_, RG_WIDTH)
                       + rg_ba.astype(f32))
    i_g = jax.nn.sigmoid(jnp.einsum('bsni,nij->bsnj', xb, rg_wx.astype(f32)).reshape(B_, S_, RG_WIDTH)
                         + rg_bx.astype(f32))
    log_a = -LRU_C * r * jax.nn.softplus(-rg_lambda.astype(f32))
    a = jnp.exp(log_a)
    b = jnp.sqrt(jnp.maximum(-jnp.expm1(2.0 * log_a), 0.0)) * (i_g * xr)
    _, h = lax.associative_scan(_linear_combine, (a, b), axis=1)
    rg_out = h * jax.nn.gelu(rg_gate.astype(f32))
    mixed = jnp.concatenate([gdn_out, rg_out], axis=-1).astype(w_out.dtype)
    return mixed @ w_out


def stick_breaking_attention(q, k, v):
    S_ = q.shape[2]
    d = q.shape[-1]
    outs = []
    for blk in range(S_ // SB_BLOCK):
        t0 = blk * SB_BLOCK
        end = t0 + SB_BLOCK
        z = jnp.einsum('bhtd,bhsd->bhts', q[:, :, t0:end], k[:, :, :end]) * d ** -0.5
        t_pos = t0 + jnp.arange(SB_BLOCK)
        s_pos = jnp.arange(end)
        mask = s_pos[None, :] < t_pos[:, None]
        log1m = jnp.where(mask, jax.nn.log_sigmoid(-z), 0.0)
        tail = lax.cumsum(log1m, axis=3, reverse=True) - log1m
        wts = jnp.where(mask, jnp.exp(jax.nn.log_sigmoid(z) + tail), 0.0)
        outs.append(jnp.einsum('bhts,bhsd->bhtd', wts, v[:, :, :end]))
    return jnp.concatenate(outs, axis=2)


def s5_ssm(u, A_re, A_im, log_dt, B_re, B_im, C_re, C_im, D):
    f32 = jnp.float32
    B_, S_, _ = u.shape
    ug = u.reshape(B_, S_, S5_GROUPS, S5_GROUP)
    dt = jnp.exp(log_dt.astype(f32))[:, None]
    lr = jnp.minimum(A_re.astype(f32), -1e-4)
    li = A_im.astype(f32)
    mag = jnp.exp(lr * dt)
    ab_re = mag * jnp.cos(li * dt)
    ab_im = mag * jnp.sin(li * dt)
    den = lr * lr + li * li
    nr = ab_re - 1.0
    ni = ab_im
    cr = (nr * lr + ni * li) / den
    ci = (ni * lr - nr * li) / den
    br_, bi_ = B_re.astype(f32), B_im.astype(f32)
    bb_re = cr[..., None] * br_ - ci[..., None] * bi_
    bb_im = cr[..., None] * bi_ + ci[..., None] * br_
    bu_re = jnp.einsum('bsgc,gnc->bsgn', ug, bb_re)
    bu_im = jnp.einsum('bsgc,gnc->bsgn', ug, bb_im)
    a_re = jnp.broadcast_to(ab_re, bu_re.shape)
    a_im = jnp.broadcast_to(ab_im, bu_re.shape)
    _, _, h_re, h_im = lax.associative_scan(_complex_combine, (a_re, a_im, bu_re, bu_im), axis=1)
    y = (jnp.einsum('bsgn,gcn->bsgc', h_re, C_re.astype(f32))
         - jnp.einsum('bsgn,gcn->bsgc', h_im, C_im.astype(f32)))
    return y.reshape(B_, S_, S5_WIDTH) + D.astype(f32) * u


def odd_mixer(x, w_in, s5_A_re, s5_A_im, s5_log_dt, s5_B_re, s5_B_im, s5_C_re, s5_C_im,
              s5_D, s5_glu_w, s5_glu_b, w_out):
    f32 = jnp.float32
    B_, S_, _ = x.shape
    proj = x @ w_in
    q, k, v, u = _split(proj, [SB_WIDTH, SB_WIDTH, SB_WIDTH, S5_WIDTH])
    heads = lambda t: t.astype(f32).reshape(B_, S_, SB_HEADS, HEAD_DIM).transpose(0, 2, 1, 3)
    sb = stick_breaking_attention(heads(q), heads(k), heads(v))
    sb_out = sb.transpose(0, 2, 1, 3).reshape(B_, S_, SB_WIDTH)
    y = s5_ssm(u.astype(f32), s5_A_re, s5_A_im, s5_log_dt, s5_B_re, s5_B_im, s5_C_re, s5_C_im, s5_D)
    yg = jax.nn.gelu(y)
    s5_out = yg * jax.nn.sigmoid(yg @ s5_glu_w.astype(f32) + s5_glu_b.astype(f32))
    mixed = jnp.concatenate([sb_out, s5_out], axis=-1).astype(w_out.dtype)
    return mixed @ w_out


def peer_ffn(x, wq, keys, U, V):
    B_, S_, Dm = x.shape
    T = B_ * S_
    xt = x.reshape(T, Dm)
    q = (xt @ wq).astype(jnp.float32).reshape(T, PEER_HEADS, 2, PEER_HALF)
    kf = keys.astype(jnp.float32)
    s1 = jnp.einsum('thd,hnd->thn', q[:, :, 0], kf[:, 0])
    s2 = jnp.einsum('thd,hnd->thn', q[:, :, 1], kf[:, 1])
    v1, i1 = lax.top_k(s1, PEER_TOPK)
    v2, i2 = lax.top_k(s2, PEER_TOPK)
    cand = (v1[..., :, None] + v2[..., None, :]).reshape(T, PEER_HEADS, PEER_TOPK * PEER_TOPK)
    sc, pos = lax.top_k(cand, PEER_TOPK)
    expert = (jnp.take_along_axis(i1, pos // PEER_TOPK, axis=-1) * PEER_NKEYS
              + jnp.take_along_axis(i2, pos % PEER_TOPK, axis=-1))
    gate = jax.nn.softmax(sc, axis=-1)
    nb = T // PEER_TOKEN_BLOCK

    def block_fn(args):
        xb, eb, gb = args
        act = jax.nn.gelu(jnp.einsum('phkd,pd->phk', U[eb], xb).astype(jnp.float32))
        wts = (gb * act).astype(V.dtype)
        return jnp.einsum('phk,phkd->pd', wts, V[eb])

    out = lax.map(block_fn, (xt.reshape(nb, PEER_TOKEN_BLOCK, Dm),
                             expert.reshape(nb, PEER_TOKEN_BLOCK, PEER_HEADS, PEER_TOPK),
                             gate.reshape(nb, PEER_TOKEN_BLOCK, PEER_HEADS, PEER_TOPK)))
    return out.reshape(B_, S_, Dm)


def setup_inputs(seed: int = 0) -> dict:
    key = jax.random.key(seed)
    ks = iter(jax.random.split(key, 48))
    f32 = jnp.float32
    nrm = lambda shape, std: jax.random.normal(next(ks), shape, f32) * std
    unif = lambda shape, lo, hi: jax.random.uniform(next(ks), shape, f32, lo, hi)
    dt_g = jnp.exp(unif((N_EVEN, GDN_HEADS), math.log(1e-3), math.log(1e-1)))
    p_lru = unif((N_EVEN, RG_WIDTH), 0.9, 0.999) ** (1.0 / LRU_C)
    return {
        'x': nrm((BATCH, SEQ, D_MODEL), 1.0),
        'w_in_e': nrm((N_EVEN, D_MODEL, EVEN_IN), D_MODEL ** -0.5),
        'gdn_conv_w': nrm((N_EVEN, CONV_W, 3 * GDN_WIDTH), CONV_W ** -0.5),
        'gdn_A_log': jnp.log(unif((N_EVEN, GDN_HEADS), 1.0, 16.0)),
        'gdn_dt_bias': dt_g + jnp.log(-jnp.expm1(-dt_g)),
        'gdn_norm_w': 1.0 + nrm((N_EVEN, HEAD_DIM), 0.01),
        'rg_conv_w': nrm((N_EVEN, CONV_W, RG_WIDTH), CONV_W ** -0.5),
        'rg_conv_b': nrm((N_EVEN, RG_WIDTH), 0.01),
        'rg_wa': nrm((N_EVEN, RG_BLOCKS, RG_BLOCK, RG_BLOCK), RG_BLOCK ** -0.5),
        'rg_ba': nrm((N_EVEN, RG_WIDTH), 0.01),
        'rg_wx': nrm((N_EVEN, RG_BLOCKS, RG_BLOCK, RG_BLOCK), RG_BLOCK ** -0.5),
        'rg_bx': nrm((N_EVEN, RG_WIDTH), 0.01),
        'rg_lambda': jnp.log(p_lru) - jnp.log1p(-p_lru),
        'w_out_e': nrm((N_EVEN, EVEN_MIX, D_MODEL), EVEN_MIX ** -0.5 * DN_BETA),
        'w_in_o': nrm((N_ODD, D_MODEL, ODD_IN), D_MODEL ** -0.5),
        's5_A_re': -0.5 + nrm((N_ODD, S5_GROUPS, S5_STATE), 0.01),
        's5_A_im': math.pi * jnp.arange(S5_STATE, dtype=f32) + nrm((N_ODD, S5_GROUPS, S5_STATE), 0.01),
        's5_log_dt': unif((N_ODD, S5_GROUPS), math.log(1e-3), math.log(1e-1)),
        's5_B_re': nrm((N_ODD, S5_GROUPS, S5_STATE, S5_GROUP), (2 * S5_GROUP) ** -0.5),
        's5_B_im': nrm((N_ODD, S5_GROUPS, S5_STATE, S5_GROUP), (2 * S5_GROUP) ** -0.5),
        's5_C_re': nrm((N_ODD, S5_GROUPS, S5_GROUP, S5_STATE), 0.5),
        's5_C_im': nrm((N_ODD, S5_GROUPS, S5_GROUP, S5_STATE), 0.5),
        's5_D': nrm((N_ODD, S5_WIDTH), 0.5),
        's5_glu_w': nrm((N_ODD, S5_WIDTH, S5_WIDTH), S5_WIDTH ** -0.5),
        's5_glu_b': nrm((N_ODD, S5_WIDTH), 0.01),
        'w_out_o': nrm((N_ODD, ODD_MIX, D_MODEL), ODD_MIX ** -0.5 * DN_BETA),
        'ln_mix_g': 1.0 + nrm((DEPTH, D_MODEL), 0.01),
        'ln_mix_b': nrm((DEPTH, D_MODEL), 0.01),
        'peer_wq': nrm((DEPTH, D_MODEL, PEER_HEADS * PEER_QDIM), D_MODEL ** -0.5),
        'peer_keys': nrm((DEPTH, PEER_HEADS, 2, PEER_NKEYS, PEER_HALF), PEER_HALF ** -0.5),
        'peer_u': nrm((DEPTH, PEER_EXPERTS, D_MODEL), D_MODEL ** -0.5),
        'peer_v': nrm((DEPTH, PEER_EXPERTS, D_MODEL), DN_BETA * PEER_HEADS ** -0.5),
        'ln_ffn_g': 1.0 + nrm((DEPTH, D_MODEL), 0.01),
        'ln_ffn_b': nrm((DEPTH, D_MODEL), 0.01),
    }


def reference(x, w_in_e, gdn_conv_w, gdn_A_log, gdn_dt_bias, gdn_norm_w, rg_conv_w, rg_conv_b,
              rg_wa, rg_ba, rg_wx, rg_bx, rg_lambda, w_out_e, w_in_o, s5_A_re, s5_A_im, s5_log_dt,
              s5_B_re, s5_B_im, s5_C_re, s5_C_im, s5_D, s5_glu_w, s5_glu_b, w_out_o,
              ln_mix_g, ln_mix_b, peer_wq, peer_keys, peer_u, peer_v, ln_ffn_g, ln_ffn_b):
    h = x
    for layer in range(DEPTH):
        j = layer // 2
        if layer % 2 == 0:
            mix = even_mixer(h, w_in_e[j], gdn_conv_w[j], gdn_A_log[j], gdn_dt_bias[j], gdn_norm_w[j],
                             rg_conv_w[j], rg_conv_b[j], rg_wa[j], rg_ba[j], rg_wx[j], rg_bx[j],
                             rg_lambda[j], w_out_e[j])
        else:
            mix = odd_mixer(h, w_in_o[j], s5_A_re[j], s5_A_im[j], s5_log_dt[j], s5_B_re[j], s5_B_im[j],
                            s5_C_re[j], s5_C_im[j], s5_D[j], s5_glu_w[j], s5_glu_b[j], w_out_o[j])
        h = layer_norm(DN_ALPHA * h + mix, ln_mix_g[layer], ln_mix_b[layer]).astype(x.dtype)
        ffn = peer_ffn(h, peer_wq[layer], peer_keys[layer], peer_u[layer], peer_v[layer])
        h = layer_norm(DN_ALPHA * h + ffn, ln_ffn_g[layer], ln_ffn_b[layer]).astype(x.dtype)
    return h
```

```python
import functools
import math

import jax
import jax.numpy as jnp
from jax import lax
from jax.experimental import pallas as pl
from jax.experimental.pallas import tpu as pltpu

F32 = jnp.float32
BF16 = jnp.bfloat16
HIGHEST = lax.Precision.HIGHEST

LANES = 128
SUBLANES = 8
VMEM_BYTES_V7X = 64 * 1024 * 1024
VMEM_LIMIT = VMEM_BYTES_V7X - 8 * 1024 * 1024

HEAD_DIM = 128
CONV_W = 4
CONV_PAD = SUBLANES
GDN_CHUNK = 64
RG_BLOCK = 128
LRU_C = 8.0
S5_GROUP = 16
S5_STATE = 64
PEER_TOPK = 16
DEPTH = 2
DN_ALPHA = (2.0 * DEPTH) ** 0.25
LN_EPS = 1e-5
RMS_EPS = 1e-6
NEG_BIG = -3.0e38


def _params(sem, vmem=VMEM_LIMIT):
    return pltpu.CompilerParams(dimension_semantics=sem, vmem_limit_bytes=vmem)


def _softplus(x):
    return jnp.maximum(x, 0.0) + jnp.log1p(jnp.exp(-jnp.abs(x)))


def _sigmoid(x):
    return 1.0 / (1.0 + jnp.exp(-x))


def _gelu(x):
    return 0.5 * x * (1.0 + jnp.tanh(0.7978845608028654 * (x + 0.044715 * (x * x * x))))


def _dot(a, b, precision=None):
    return jnp.dot(a, b, preferred_element_type=F32, precision=precision)


def _dot_nt(a, b, precision=None):
    return lax.dot_general(a, b, (((1,), (1,)), ((), ())), preferred_element_type=F32, precision=precision)


def _dot_tn(a, b, precision=None):
    return lax.dot_general(a, b, (((0,), (0,)), ((), ())), preferred_element_type=F32, precision=precision)


def _mm_body(a_ref, b_ref, o_ref, acc_ref):
    k = pl.program_id(2)

    @pl.when(k == 0)
    def _():
        acc_ref[...] = jnp.zeros_like(acc_ref)

    acc_ref[...] += _dot(a_ref[...].astype(BF16), b_ref[...])

    @pl.when(k == pl.num_programs(2) - 1)
    def _():
        o_ref[...] = acc_ref[...].astype(o_ref.dtype)


def _matmul(a, b, out_dtype=F32, tm=1024, tn=1024, tk=1024):
    m, kd = a.shape
    _, n = b.shape
    tm, tn, tk = min(tm, m), min(tn, n), min(tk, kd)
    assert m % tm == 0 and n % tn == 0 and kd % tk == 0, (a.shape, b.shape)
    return pl.pallas_call(
        _mm_body,
        grid=(m // tm, n // tn, kd // tk),
        in_specs=[pl.BlockSpec((tm, tk), lambda i, j, k: (i, k)),
                  pl.BlockSpec((tk, tn), lambda i, j, k: (k, j))],
        out_specs=pl.BlockSpec((tm, tn), lambda i, j, k: (i, j)),
        out_shape=jax.ShapeDtypeStruct((m, n), out_dtype),
        scratch_shapes=[pltpu.VMEM((tm, tn), F32)],
        compiler_params=_params(("parallel", "parallel", "arbitrary")),
    )(a, b)


def _ln_body(h_ref, d_ref, g_ref, b_ref, o_ref, ob_ref):
    y = DN_ALPHA * h_ref[...] + d_ref[...]
    mu = jnp.mean(y, axis=-1, keepdims=True)
    yc = y - mu
    var = jnp.mean(yc * yc, axis=-1, keepdims=True)
    out = yc * lax.rsqrt(var + LN_EPS) * g_ref[...] + b_ref[...]
    o_ref[...] = out
    ob_ref[...] = out.astype(BF16)


def _residual_ln(h, delta, g, b, tm=256):
    t, d = h.shape
    tm = min(tm, t)
    row = pl.BlockSpec((tm, d), lambda i: (i, 0))
    vec = pl.BlockSpec((1, d), lambda i: (0, 0))
    return pl.pallas_call(
        _ln_body,
        grid=(t // tm,),
        in_specs=[row, row, vec, vec],
        out_specs=[row, row],
        out_shape=[jax.ShapeDtypeStruct((t, d), F32), jax.ShapeDtypeStruct((t, d), BF16)],
        compiler_params=_params(("parallel",)),
    )(h, delta, g.reshape(1, d), b.reshape(1, d))


def _conv_rows(xpad, w, r, rows):
    acc = None
    for k in range(CONV_W):
        tap = xpad[pl.ds(r + CONV_PAD - (CONV_W - 1) + k, rows), :] * w[k:k + 1, :]
        acc = tap if acc is None else acc + tap
    return acc


def _row_tile(s):
    return min(s, 256)


def _gdn_body(q_ref, k_ref, v_ref, z_ref, wq_ref, wk_ref, wv_ref, acol_ref, bcol_ref, arow_ref,
              alog_ref, dtb_ref, nw_ref, o_ref, xpad, qs, ks, vs, gcol_s, beta_s, grow_s):
    s = q_ref.shape[1]
    c = GDN_CHUNK
    rt = _row_tile(s)
    xpad[0:CONV_PAD, :] = jnp.zeros((CONV_PAD, HEAD_DIM), F32)
    for src, w_ref, dst, kind in ((q_ref, wq_ref, qs, "q"), (k_ref, wk_ref, ks, "k"), (v_ref, wv_ref, vs, "v")):
        xpad[CONV_PAD:CONV_PAD + s, :] = src[0]
        w = w_ref[...]
        for r in range(0, s, rt):
            y = _conv_rows(xpad, w, r, rt)
            y = y * _sigmoid(y)
            if kind != "v":
                y = y * lax.rsqrt(jnp.sum(y * y, axis=-1, keepdims=True) + RMS_EPS)
            if kind == "q":
                y = y * (HEAD_DIM ** -0.5)
            dst[r:r + rt, :] = y

    neg_a = -jnp.exp(alog_ref[0])
    dtb = dtb_ref[0]
    gcol_s[...] = neg_a * _softplus(acol_ref[0, 0] + dtb)
    grow_s[...] = neg_a * _softplus(arow_ref[0, 0] + dtb)
    beta_s[...] = _sigmoid(bcol_ref[0, 0])
    nw = nw_ref[...]

    ri = lax.broadcasted_iota(jnp.int32, (c, c), 0)
    ci = lax.broadcasted_iota(jnp.int32, (c, c), 1)
    lower = ri >= ci
    eye = (ri == ci).astype(F32)

    def chunk(i, st):
        r0 = pl.multiple_of(i * c, c)
        qc = qs[pl.ds(r0, c), :]
        kc = ks[pl.ds(r0, c), :]
        vc = vs[pl.ds(r0, c), :]
        gcol = gcol_s[pl.ds(r0, c), :]
        beta = beta_s[pl.ds(r0, c), :]
        grow = grow_s[pl.ds(i, 1), :]
        gc_col = jnp.sum(jnp.where(lower, grow, 0.0), axis=1, keepdims=True)
        gc_row = jnp.sum(jnp.where(ri <= ci, gcol, 0.0), axis=0, keepdims=True)
        gamma = jnp.where(lower, jnp.exp(jnp.where(lower, gc_col - gc_row, 0.0)), 0.0)
        kb = kc * beta
        neg_m = jnp.where(ri > ci, -(_dot_nt(kb, kc, HIGHEST) * gamma), 0.0)
        t_inv = eye + neg_m
        p = neg_m
        for _ in range(int(math.log2(c)) - 1):
            p = _dot(p, p, HIGHEST)
            t_inv = t_inv + _dot(t_inv, p, HIGHEST)
        eg = jnp.exp(gc_col)
        u = _dot(t_inv, vc * beta, HIGHEST)
        w = _dot(t_inv, kb * eg, HIGHEST)
        a_qk = _dot_nt(qc, kc, HIGHEST) * gamma
        g_last = gc_row[:, c - 1:c]
        k_tail = kc * jnp.exp(g_last - gc_col)
        v_new = u - _dot(w, st, HIGHEST)
        o = _dot(qc * eg, st, HIGHEST) + _dot(a_qk, v_new, HIGHEST)
        st = st * jnp.exp(g_last) + _dot_tn(k_tail, v_new, HIGHEST)
        o = o * lax.rsqrt(jnp.mean(o * o, axis=-1, keepdims=True) + RMS_EPS) * nw
        zz = z_ref[0, pl.ds(r0, c), :]
        o_ref[0, pl.ds(r0, c), :] = (o * (zz * _sigmoid(zz))).astype(o_ref.dtype)
        return st

    lax.fori_loop(0, s // c, chunk, jnp.zeros((HEAD_DIM, HEAD_DIM), F32))


def _gdn(qkvz, ab, conv_w, a_log, dt_bias, norm_w, heads):
    b, s, _ = qkvz.shape
    c = GDN_CHUNK
    a_t = jnp.transpose(ab[:, :, :heads], (0, 2, 1))
    b_t = jnp.transpose(ab[:, :, heads:2 * heads], (0, 2, 1))
    acol = a_t.reshape(b, heads, s, 1)
    bcol = b_t.reshape(b, heads, s, 1)
    arow = a_t.reshape(b, heads, s // c, c)
    head = lambda off: pl.BlockSpec((1, s, HEAD_DIM), lambda i, j: (i, 0, off + j))
    cw = lambda off: pl.BlockSpec((CONV_W, HEAD_DIM), lambda i, j: (0, off + j))
    col = pl.BlockSpec((1, 1, s, 1), lambda i, j: (i, j, 0, 0))
    scal = pl.BlockSpec((1, 1, 1), lambda i, j: (j, 0, 0))
    return pl.pallas_call(
        _gdn_body,
        grid=(b, heads),
        in_specs=[head(0), head(heads), head(2 * heads), head(3 * heads),
                  cw(0), cw(heads), cw(2 * heads), col, col,
                  pl.BlockSpec((1, 1, s // c, c), lambda i, j: (i, j, 0, 0)),
                  scal, scal, pl.BlockSpec((1, HEAD_DIM), lambda i, j: (0, 0))],
        out_specs=pl.BlockSpec((1, s, HEAD_DIM), lambda i, j: (i, 0, j)),
        out_shape=jax.ShapeDtypeStruct((b, s, heads * HEAD_DIM), BF16),
        scratch_shapes=[pltpu.VMEM((s + CONV_PAD, HEAD_DIM), F32)] + [pltpu.VMEM((s, HEAD_DIM), F32)] * 3
        + [pltpu.VMEM((s, 1), F32), pltpu.VMEM((s, 1), F32), pltpu.VMEM((s // c, c), F32)],
        compiler_params=_params(("parallel", "parallel")),
    )(qkvz, qkvz, qkvz, qkvz, conv_w, conv_w, conv_w, acol, bcol, arow,
      a_log.reshape(heads, 1, 1), dt_bias.reshape(heads, 1, 1), norm_w.reshape(1, HEAD_DIM))


def _shift_rows(x, d, fill):
    row = lax.broadcasted_iota(jnp.int32, x.shape, 0)
    return jnp.where(row >= d, pltpu.roll(x, d, 0), fill)


def _rg_body(x_ref, gate_ref, cw_ref, cb_ref, wa_ref, wx_ref, ba_ref, bx_ref, lam_ref, o_ref,
             xpad, a_s, h_s):
    s = x_ref.shape[1]
    cb = x_ref.shape[2]
    nb = cb // RG_BLOCK
    rt = _row_tile(s)
    xpad[0:CONV_PAD, :] = jnp.zeros((CONV_PAD, cb), F32)
    xpad[CONV_PAD:CONV_PAD + s, :] = x_ref[0]
    w = cw_ref[...]
    bias = cb_ref[...]
    lam_sp = _softplus(-lam_ref[...])
    for r in range(0, s, rt):
        xr = _conv_rows(xpad, w, r, rt) + bias
        for n in range(nb):
            sl = slice(n * RG_BLOCK, (n + 1) * RG_BLOCK)
            xb = xr[:, sl]
            xb16 = xb.astype(BF16)
            rgate = _sigmoid(_dot(xb16, wa_ref[n]) + ba_ref[:, sl])
            igate = _sigmoid(_dot(xb16, wx_ref[n]) + bx_ref[:, sl])
            log_a = -LRU_C * rgate * lam_sp[:, sl]
            a = jnp.exp(log_a)
            a_s[r:r + rt, sl] = a
            h_s[r:r + rt, sl] = jnp.sqrt(jnp.maximum(1.0 - a * a, 0.0)) * (igate * xb)

    def step(i, carry):
        r0 = pl.multiple_of(i * SUBLANES, SUBLANES)
        a = a_s[pl.ds(r0, SUBLANES), :]
        bv = h_s[pl.ds(r0, SUBLANES), :]
        for d in (1, 2, 4):
            bv = a * _shift_rows(bv, d, 0.0) + bv
            a = a * _shift_rows(a, d, 1.0)
        h = bv + a * carry
        h_s[pl.ds(r0, SUBLANES), :] = h
        return h[SUBLANES - 1:SUBLANES, :]

    lax.fori_loop(0, s // SUBLANES, step, jnp.zeros((1, cb), F32))
    for r in range(0, s, rt):
        o_ref[0, r:r + rt, :] = (h_s[r:r + rt, :] * _gelu(gate_ref[0, r:r + rt, :])).astype(o_ref.dtype)


def _rg_lru(rg, conv_w, conv_b, wa, wx, ba, bx, lam, blocks_per_step=4):
    b, s, w2 = rg.shape
    width = w2 // 2
    nblk = width // RG_BLOCK
    nb = min(blocks_per_step, nblk)
    cb = nb * RG_BLOCK
    steps = width // cb
    vec = pl.BlockSpec((1, cb), lambda i, j: (0, j))
    wblk = pl.BlockSpec((nb, RG_BLOCK, RG_BLOCK), lambda i, j: (j, 0, 0))
    return pl.pallas_call(
        _rg_body,
        grid=(b, steps),
        in_specs=[pl.BlockSpec((1, s, cb), lambda i, j: (i, 0, j)),
                  pl.BlockSpec((1, s, cb), lambda i, j: (i, 0, steps + j)),
                  pl.BlockSpec((CONV_W, cb), lambda i, j: (0, j)), vec, wblk, wblk, vec, vec, vec],
        out_specs=pl.BlockSpec((1, s, cb), lambda i, j: (i, 0, j)),
        out_shape=jax.ShapeDtypeStruct((b, s, width), BF16),
        scratch_shapes=[pltpu.VMEM((s + CONV_PAD, cb), F32), pltpu.VMEM((s, cb), F32), pltpu.VMEM((s, cb), F32)],
        compiler_params=_params(("parallel", "parallel")),
    )(rg, rg, conv_w, conv_b.reshape(1, width), wa.astype(BF16), wx.astype(BF16),
      ba.reshape(1, width), bx.reshape(1, width), lam.reshape(1, width))


def _sb_body(q_ref, k_ref, v_ref, o_ref, *, bk):
    bq = q_ref.shape[1]
    qi = pl.program_id(2)
    q16 = (q_ref[0] * (HEAD_DIM ** -0.5)).astype(BF16)
    t_pos = qi * bq + lax.broadcasted_iota(jnp.int32, (bq, bk), 0)
    s_off = lax.broadcasted_iota(jnp.int32, (bq, bk), 1)
    tri = (lax.broadcasted_iota(jnp.int32, (bk, bk), 0) > lax.broadcasted_iota(jnp.int32, (bk, bk), 1)).astype(BF16)
    nkb = (qi + 1) * (bq // bk)

    def body(i, carry):
        acc, run = carry
        r0 = pl.multiple_of((nkb - 1 - i) * bk, bk)
        kblk = k_ref[0, pl.ds(r0, bk), :].astype(BF16)
        vblk = v_ref[0, pl.ds(r0, bk), :].astype(BF16)
        z = _dot_nt(q16, kblk)
        sp = _softplus(z)
        valid = (s_off + r0) < t_pos
        lm = jnp.where(valid, -sp, 0.0)
        hi = lm.astype(BF16)
        lo = (lm - hi.astype(F32)).astype(BF16)
        tail = _dot(hi, tri) + _dot(lo, tri) + run
        wts = jnp.where(valid, jnp.exp(z - sp + tail), 0.0)
        acc = acc + _dot(wts.astype(BF16), vblk)
        run = run + jnp.sum(lm, axis=1, keepdims=True)
        return acc, run

    acc, _ = lax.fori_loop(0, nkb, body, (jnp.zeros((bq, HEAD_DIM), F32), jnp.zeros((bq, 1), F32)))
    o_ref[0] = acc.astype(o_ref.dtype)


def _stick_breaking(proj, heads, bq=256, bk=128):
    b, s, _ = proj.shape
    bq = min(bq, s)
    bk = min(bk, bq)
    return pl.pallas_call(
        functools.partial(_sb_body, bk=bk),
        grid=(b, heads, s // bq),
        in_specs=[pl.BlockSpec((1, bq, HEAD_DIM), lambda i, j, t: (i, t, j)),
                  pl.BlockSpec((1, s, HEAD_DIM), lambda i, j, t: (i, 0, heads + j)),
                  pl.BlockSpec((1, s, HEAD_DIM), lambda i, j, t: (i, 0, 2 * heads + j))],
        out_specs=pl.BlockSpec((1, bq, HEAD_DIM), lambda i, j, t: (i, t, j)),
        out_shape=jax.ShapeDtypeStruct((b, s, heads * HEAD_DIM), BF16),
        compiler_params=_params(("parallel", "parallel", "arbitrary")),
    )(proj, proj, proj)


def _s5_prep_body(are_ref, aim_ref, ldt_ref, abre_ref, abim_ref, cr_ref, ci_ref):
    dt = jnp.exp(ldt_ref[...])
    lr = jnp.minimum(are_ref[...], -1e-4)
    li = aim_ref[...]
    mag = jnp.exp(lr * dt)
    ab_re = mag * jnp.cos(li * dt)
    ab_im = mag * jnp.sin(li * dt)
    den = lr * lr + li * li
    nr = ab_re - 1.0
    cr_ref[...] = (nr * lr + ab_im * li) / den
    ci_ref[...] = (ab_im * lr - nr * li) / den
    p_re, p_im = ab_re, ab_im
    for k in range(SUBLANES):
        abre_ref[k] = p_re
        abim_ref[k] = p_im
        p_re, p_im = p_re * ab_re - p_im * ab_im, p_re * ab_im + p_im * ab_re


def _s5_prep(a_re, a_im, log_dt):
    g, n = a_re.shape
    full = pl.BlockSpec((g, n), lambda: (0, 0))
    pw = pl.BlockSpec((SUBLANES, g, n), lambda: (0, 0, 0))
    return pl.pallas_call(
        _s5_prep_body,
        in_specs=[full, full, pl.BlockSpec((g, 1), lambda: (0, 0))],
        out_specs=[pw, pw, full, full],
        out_shape=[jax.ShapeDtypeStruct((SUBLANES, g, n), F32)] * 2 + [jax.ShapeDtypeStruct((g, n), F32)] * 2,
    )(a_re, a_im, log_dt.reshape(g, 1))


def _s5_body(u_ref, bre_ref, bim_ref, cre_ref, cim_ref, pre_ref, pim_ref, cr_ref, ci_ref, d_ref, o_ref,
             hre_s, him_s):
    s = u_ref.shape[1]
    rt = _row_tile(s)
    cr = cr_ref[...]
    ci = ci_ref[...]
    for r in range(0, s, rt):
        u = u_ref[0, r:r + rt, :]
        xr = _dot(u, bre_ref[0], HIGHEST)
        xi = _dot(u, bim_ref[0], HIGHEST)
        hre_s[r:r + rt, :] = cr * xr - ci * xi
        him_s[r:r + rt, :] = cr * xi + ci * xr
    p_re = pre_ref[...]
    p_im = pim_ref[...]

    def step(i, carry):
        c_re, c_im = carry
        r0 = pl.multiple_of(i * SUBLANES, SUBLANES)
        x_re = hre_s[pl.ds(r0, SUBLANES), :]
        x_im = him_s[pl.ds(r0, SUBLANES), :]
        for d in (1, 2, 4):
            a_re = p_re[d - 1:d, :]
            a_im = p_im[d - 1:d, :]
            s_re = _shift_rows(x_re, d, 0.0)
            s_im = _shift_rows(x_im, d, 0.0)
            x_re, x_im = x_re + a_re * s_re - a_im * s_im, x_im + a_re * s_im + a_im * s_re
        h_re = x_re + p_re * c_re - p_im * c_im
        h_im = x_im + p_re * c_im + p_im * c_re
        hre_s[pl.ds(r0, SUBLANES), :] = h_re
        him_s[pl.ds(r0, SUBLANES), :] = h_im
        return h_re[SUBLANES - 1:SUBLANES, :], h_im[SUBLANES - 1:SUBLANES, :]

    zero = jnp.zeros((1, hre_s.shape[1]), F32)
    lax.fori_loop(0, s // SUBLANES, step, (zero, zero))
    dvec = d_ref[...]
    for r in range(0, s, rt):
        y = _dot(hre_s[r:r + rt, :], cre_ref[0], HIGHEST) - _dot(him_s[r:r + rt, :], cim_ref[0], HIGHEST)
        y = y + dvec * u_ref[0, r:r + rt, :]
        o_ref[0, r:r + rt, :] = _gelu(y)


def _block_diag(x, nblk):
    g = x.shape[0] // nblk
    r, c = x.shape[1:]
    eye = jnp.eye(g, dtype=x.dtype)
    return jnp.einsum("jgrc,gh->jgrhc", x.reshape(nblk, g, r, c), eye).reshape(nblk, g * r, g * c)


def _s5(proj, col0, a_re, a_im, log_dt, b_re, b_im, c_re, c_im, dvec):
    b, s, _ = proj.shape
    g, n = a_re.shape
    width = g * S5_GROUP
    gpb = LANES // S5_GROUP
    nblk = g // gpb
    nst = gpb * n
    p_re, p_im, cr, ci = _s5_prep(a_re, a_im, log_dt)
    bmat_re = _block_diag(jnp.transpose(b_re, (0, 2, 1)), nblk)
    bmat_im = _block_diag(jnp.transpose(b_im, (0, 2, 1)), nblk)
    cmat_re = _block_diag(jnp.transpose(c_re, (0, 2, 1)), nblk)
    cmat_im = _block_diag(jnp.transpose(c_im, (0, 2, 1)), nblk)
    ublk = col0 // LANES
    st_row = lambda rows: pl.BlockSpec((rows, nst), lambda i, j: (0, j))
    return pl.pallas_call(
        _s5_body,
        grid=(b, nblk),
        in_specs=[pl.BlockSpec((1, s, LANES), lambda i, j: (i, 0, ublk + j)),
                  pl.BlockSpec((1, LANES, nst), lambda i, j: (j, 0, 0)),
                  pl.BlockSpec((1, LANES, nst), lambda i, j: (j, 0, 0)),
                  pl.BlockSpec((1, nst, LANES), lambda i, j: (j, 0, 0)),
                  pl.BlockSpec((1, nst, LANES), lambda i, j: (j, 0, 0)),
                  st_row(SUBLANES), st_row(SUBLANES), st_row(1), st_row(1),
                  pl.BlockSpec((1, LANES), lambda i, j: (0, j))],
        out_specs=pl.BlockSpec((1, s, LANES), lambda i, j: (i, 0, j)),
        out_shape=jax.ShapeDtypeStruct((b, s, width), F32),
        scratch_shapes=[pltpu.VMEM((s, nst), F32), pltpu.VMEM((s, nst), F32)],
        compiler_params=_params(("parallel", "parallel")),
    )(proj, bmat_re, bmat_im, cmat_re, cmat_im, p_re.reshape(SUBLANES, g * n), p_im.reshape(SUBLANES, g * n),
      cr.reshape(1, g * n), ci.reshape(1, g * n), dvec.reshape(1, width))


def _glu_body(y_ref, w_ref, b_ref, o_ref):
    y = y_ref[...]
    o_ref[...] = (y * _sigmoid(_dot(y.astype(BF16), w_ref[...]) + b_ref[...])).astype(o_ref.dtype)


def _glu(y, w, bias, tm=512):
    t, d = y.shape
    tm = min(tm, t)
    return pl.pallas_call(
        _glu_body,
        grid=(t // tm,),
        in_specs=[pl.BlockSpec((tm, d), lambda i: (i, 0)), pl.BlockSpec((d, d), lambda i: (0, 0)),
                  pl.BlockSpec((1, d), lambda i: (0, 0))],
        out_specs=pl.BlockSpec((tm, d), lambda i: (i, 0)),
        out_shape=jax.ShapeDtypeStruct((t, d), BF16),
        compiler_params=_params(("parallel",)),
    )(y, w.astype(BF16), bias.reshape(1, d))


def _top_rows(s, k):
    rows = []
    for _ in range(k):
        m = jnp.max(s, axis=0, keepdims=True)
        rows.append(m)
        s = jnp.where(s == m, NEG_BIG, s)
    return rows


def _peer_route_body(q_ref, keys_ref, d1_ref, e1_ref, s2_ref, e2_ref):
    half = keys_ref.shape[3]
    q = q_ref[...]
    s1 = _dot_nt(keys_ref[0, 0], q[:, :half], HIGHEST)
    s2 = _dot_nt(keys_ref[0, 1], q[:, half:], HIGHEST)
    nk = PEER_TOPK + 1
    v1 = _top_rows(s1, nk)
    pad = jnp.full((-nk % SUBLANES, q.shape[0]), NEG_BIG, F32)
    v2 = jnp.concatenate(_top_rows(s2, nk) + [pad], axis=0)
    cand = jnp.concatenate([r + v2 for r in v1], axis=0)
    top = _top_rows(cand, nk)
    zsum = jnp.ones_like(top[0])
    for r in top[1:PEER_TOPK]:
        zsum = zsum + jnp.exp(r - top[0])
    thr = 0.5 * (top[PEER_TOPK - 1] + top[PEER_TOPK])
    d1_ref[0] = thr - s1
    e1_ref[0] = jnp.exp(s1 - v1[0])
    s2_ref[0] = s2
    e2_ref[0] = jnp.exp(s2 - v2[0:1, :]) / zsum


def _peer_route(q, keys, tt=512):
    t = q.shape[0]
    heads, _, nkeys, half = keys.shape
    tt = min(tt, t)
    table = pl.BlockSpec((1, nkeys, tt), lambda i, h: (h, 0, i))
    shape = jax.ShapeDtypeStruct((heads, nkeys, t), F32)
    return pl.pallas_call(
        _peer_route_body,
        grid=(t // tt, heads),
        in_specs=[pl.BlockSpec((tt, 2 * half), lambda i, h: (i, h)),
                  pl.BlockSpec((1, 2, nkeys, half), lambda i, h: (h, 0, 0, 0))],
        out_specs=[table] * 4,
        out_shape=[shape] * 4,
        compiler_params=_params(("parallel", "parallel")),
    )(q, keys)


def _peer_dense_body(x_ref, u_ref, v_ref, d1_ref, e1_ref, s2_ref, e2_ref, o_ref):
    heads, nkeys, _ = s2_ref.shape
    te = u_ref.shape[0]
    j = pl.program_id(1)

    @pl.when(j == 0)
    def _():
        o_ref[...] = jnp.zeros_like(o_ref)

    act = _gelu(_dot_nt(u_ref[...], x_ref[...]))
    parts = []
    for r in range(te // nkeys):
        i1 = j * (te // nkeys) + r
        w = None
        for h in range(heads):
            d1 = d1_ref[h, pl.ds(i1, 1), :]
            e1 = e1_ref[h, pl.ds(i1, 1), :]
            term = jnp.where(s2_ref[h] >= d1, e2_ref[h] * e1, 0.0)
            w = term if w is None else w + term
        parts.append(w * act[r * nkeys:(r + 1) * nkeys, :])
    p_t = jnp.concatenate(parts, axis=0) if len(parts) > 1 else parts[0]
    o_ref[...] += _dot(p_t.T.astype(BF16), v_ref[...])


def _peer_dense(x16, u16, v16, tables, tt=512, te=512):
    t, d = x16.shape
    e = u16.shape[0]
    heads, nkeys, _ = tables[0].shape
    tt = min(tt, t)
    te = min(te, e)
    assert te % nkeys == 0 and e == nkeys * nkeys
    once = pl.Buffered(1)
    table = pl.BlockSpec((heads, nkeys, tt), lambda i, j: (0, 0, i), pipeline_mode=once)
    return pl.pallas_call(
        _peer_dense_body,
        grid=(t // tt, e // te),
        in_specs=[pl.BlockSpec((tt, d), lambda i, j: (i, 0), pipeline_mode=once),
                  pl.BlockSpec((te, d), lambda i, j: (j, 0)),
                  pl.BlockSpec((te, d), lambda i, j: (j, 0)),
                  table, table, table, table],
        out_specs=pl.BlockSpec((tt, d), lambda i, j: (i, 0)),
        out_shape=jax.ShapeDtypeStruct((t, d), F32),
        compiler_params=_params(("parallel", "arbitrary")),
    )(x16, u16, v16, *tables)


def _peer_ffn(h16, wq16, keys, u16, v16):
    q = _matmul(h16, wq16)
    return _peer_dense(h16, u16, v16, _peer_route(q, keys))


def _even_mixer(x2d, b, s, w_in, gdn_conv_w, a_log, dt_bias, norm_w, rg_conv_w, rg_conv_b,
                rg_wa, rg_ba, rg_wx, rg_bx, rg_lambda, w_out):
    heads = a_log.shape[0]
    gw = heads * HEAD_DIM
    rw = rg_lambda.shape[0]
    ab_pad = LANES - 2 * heads
    w_qkvz = w_in[:, :4 * gw].astype(BF16)
    w_ab = jnp.pad(w_in[:, 4 * gw:4 * gw + 2 * heads], ((0, 0), (0, ab_pad))).astype(BF16)
    w_rg = w_in[:, 4 * gw + 2 * heads:].astype(BF16)
    qkvz = _matmul(x2d, w_qkvz).reshape(b, s, 4 * gw)
    ab = _matmul(x2d, w_ab).reshape(b, s, LANES)
    rg = _matmul(x2d, w_rg).reshape(b, s, 2 * rw)
    gdn_out = _gdn(qkvz, ab, gdn_conv_w, a_log, dt_bias, norm_w, heads)
    rg_out = _rg_lru(rg, rg_conv_w, rg_conv_b, rg_wa, rg_wx, rg_ba, rg_bx, rg_lambda)
    mixed = jnp.concatenate([gdn_out, rg_out], axis=-1).reshape(b * s, gw + rw)
    return _matmul(mixed, w_out.astype(BF16))


def _odd_mixer(x2d, b, s, w_in, a_re, a_im, log_dt, b_re, b_im, c_re, c_im, dvec, glu_w, glu_b, w_out):
    sw = dvec.shape[0]
    sbw = (w_in.shape[1] - sw) // 3
    heads = sbw // HEAD_DIM
    proj = _matmul(x2d, w_in.astype(BF16)).reshape(b, s, 3 * sbw + sw)
    sb_out = _stick_breaking(proj, heads)
    yg = _s5(proj, 3 * sbw, a_re, a_im, log_dt, b_re, b_im, c_re, c_im, dvec)
    s5_out = _glu(yg.reshape(b * s, sw), glu_w, glu_b)
    mixed = jnp.concatenate([sb_out.reshape(b * s, sbw), s5_out], axis=-1)
    return _matmul(mixed, w_out.astype(BF16))


def kernel(x, w_in_e, gdn_conv_w, gdn_A_log, gdn_dt_bias, gdn_norm_w, rg_conv_w, rg_conv_b, rg_wa, rg_ba,
           rg_wx, rg_bx, rg_lambda, w_out_e, w_in_o, s5_A_re, s5_A_im, s5_log_dt, s5_B_re, s5_B_im,
           s5_C_re, s5_C_im, s5_D, s5_glu_w, s5_glu_b, w_out_o, ln_mix_g, ln_mix_b, peer_wq, peer_keys,
           peer_u, peer_v, ln_ffn_g, ln_ffn_b):
    b, s, d = x.shape
    depth = ln_mix_g.shape[0]
    h = x.reshape(b * s, d)
    h16 = h
    for layer in range(depth):
        j = layer // 2
        if layer % 2 == 0:
            mix = _even_mixer(h16, b, s, w_in_e[j], gdn_conv_w[j], gdn_A_log[j], gdn_dt_bias[j], gdn_norm_w[j],
                              rg_conv_w[j], rg_conv_b[j], rg_wa[j], rg_ba[j], rg_wx[j], rg_bx[j],
                              rg_lambda[j], w_out_e[j])
        else:
            mix = _odd_mixer(h16, b, s, w_in_o[j], s5_A_re[j], s5_A_im[j], s5_log_dt[j], s5_B_re[j],
                             s5_B_im[j], s5_C_re[j], s5_C_im[j], s5_D[j], s5_glu_w[j], s5_glu_b[j], w_out_o[j])
        h, h16 = _residual_ln(h, mix, ln_mix_g[layer], ln_mix_b[layer])
        ffn = _peer_ffn(h16, peer_wq[layer].astype(BF16), peer_keys[layer],
                        peer_u[layer].astype(BF16), peer_v[layer].astype(BF16))
        h, h16 = _residual_ln(h, ffn, ln_ffn_g[layer], ln_ffn_b[layer])
    return h.reshape(b, s, d)
```

```python
import functools
import math

import jax
import jax.numpy as jnp
from jax import lax
from jax.experimental import pallas as pl
from jax.experimental.pallas import tpu as pltpu

F32 = jnp.float32
BF16 = jnp.bfloat16
HIGHEST = lax.Precision.HIGHEST

LANES = 128
SUBLANES = 8
VMEM_BYTES_V7X = 64 * 1024 * 1024
VMEM_LIMIT = VMEM_BYTES_V7X - 8 * 1024 * 1024

HEAD_DIM = 128
CONV_W = 4
CONV_PAD = SUBLANES
GDN_CHUNK = 64
RG_BLOCK = 128
LRU_C = 8.0
S5_GROUP = 16
S5_STATE = 64
PEER_TOPK = 16
DEPTH = 2
DN_ALPHA = (2.0 * DEPTH) ** 0.25
LN_EPS = 1e-5
RMS_EPS = 1e-6
NEG_BIG = -3.0e38


def _params(sem, vmem=VMEM_LIMIT):
    return pltpu.CompilerParams(dimension_semantics=sem, vmem_limit_bytes=vmem)


def _softplus(x):
    return jnp.maximum(x, 0.0) + jnp.log1p(jnp.exp(-jnp.abs(x)))


def _sigmoid(x):
    return 1.0 / (1.0 + jnp.exp(-x))


def _gelu(x):
    return 0.5 * x * (1.0 + jnp.tanh(0.7978845608028654 * (x + 0.044715 * (x * x * x))))


def _dot(a, b, precision=None):
    return jnp.dot(a, b, preferred_element_type=F32, precision=precision)


def _dot_nt(a, b, precision=None):
    return lax.dot_general(a, b, (((1,), (1,)), ((), ())), preferred_element_type=F32, precision=precision)


def _dot_tn(a, b, precision=None):
    return lax.dot_general(a, b, (((0,), (0,)), ((), ())), preferred_element_type=F32, precision=precision)


def _mm_body(a_ref, b_ref, o_ref, acc_ref):
    k = pl.program_id(2)

    @pl.when(k == 0)
    def _():
        acc_ref[...] = jnp.zeros_like(acc_ref)

    acc_ref[...] += _dot(a_ref[...].astype(BF16), b_ref[...])

    @pl.when(k == pl.num_programs(2) - 1)
    def _():
        o_ref[...] = acc_ref[...].astype(o_ref.dtype)


def _mm_single_body(a_ref, b_ref, o_ref):
    o_ref[...] = _dot(a_ref[...].astype(BF16), b_ref[...]).astype(o_ref.dtype)


def _matmul(a, b, out_dtype=F32, tm=1024, tn=1024, tk=1024):
    m, kd = a.shape
    _, n = b.shape
    tm, tn = min(tm, m), min(tn, n)
    assert m % tm == 0 and n % tn == 0, (a.shape, b.shape)
    tile_bytes = 2 * (tm * kd * a.dtype.itemsize + kd * tn * b.dtype.itemsize)
    if tile_bytes <= VMEM_LIMIT // 2:
        return pl.pallas_call(
            _mm_single_body,
            grid=(m // tm, n // tn),
            in_specs=[pl.BlockSpec((tm, kd), lambda i, j: (i, 0)),
                      pl.BlockSpec((kd, tn), lambda i, j: (0, j))],
            out_specs=pl.BlockSpec((tm, tn), lambda i, j: (i, j)),
            out_shape=jax.ShapeDtypeStruct((m, n), out_dtype),
            compiler_params=_params(("parallel", "parallel")),
        )(a, b)
    tk = min(tk, kd)
    assert kd % tk == 0, (a.shape, b.shape)
    return pl.pallas_call(
        _mm_body,
        grid=(m // tm, n // tn, kd // tk),
        in_specs=[pl.BlockSpec((tm, tk), lambda i, j, k: (i, k)),
                  pl.BlockSpec((tk, tn), lambda i, j, k: (k, j))],
        out_specs=pl.BlockSpec((tm, tn), lambda i, j, k: (i, j)),
        out_shape=jax.ShapeDtypeStruct((m, n), out_dtype),
        scratch_shapes=[pltpu.VMEM((tm, tn), F32)],
        compiler_params=_params(("parallel", "parallel", "arbitrary")),
    )(a, b)


def _ln_body(h_ref, d_ref, g_ref, b_ref, o_ref, ob_ref):
    y = DN_ALPHA * h_ref[...] + d_ref[...]
    mu = jnp.mean(y, axis=-1, keepdims=True)
    yc = y - mu
    var = jnp.mean(yc * yc, axis=-1, keepdims=True)
    out = yc * lax.rsqrt(var + LN_EPS) * g_ref[...] + b_ref[...]
    o_ref[...] = out
    ob_ref[...] = out.astype(BF16)


def _residual_ln(h, delta, g, b, tm=256):
    t, d = h.shape
    tm = min(tm, t)
    row = pl.BlockSpec((tm, d), lambda i: (i, 0))
    vec = pl.BlockSpec((1, d), lambda i: (0, 0))
    return pl.pallas_call(
        _ln_body,
        grid=(t // tm,),
        in_specs=[row, row, vec, vec],
        out_specs=[row, row],
        out_shape=[jax.ShapeDtypeStruct((t, d), F32), jax.ShapeDtypeStruct((t, d), BF16)],
        compiler_params=_params(("parallel",)),
    )(h, delta, g.reshape(1, d), b.reshape(1, d))


def _conv_rows(xpad, w, r, rows):
    acc = None
    for k in range(CONV_W):
        tap = xpad[pl.ds(r + CONV_PAD - (CONV_W - 1) + k, rows), :] * w[k:k + 1, :]
        acc = tap if acc is None else acc + tap
    return acc


def _row_tile(s):
    return min(s, 256)


def _bmm(a, b):
    return jnp.einsum("gmk,gkn->gmn", a.astype(BF16), b.astype(BF16), preferred_element_type=F32)


def _bmm_nt(a, b):
    return jnp.einsum("gmk,gnk->gmn", a.astype(BF16), b.astype(BF16), preferred_element_type=F32)


def _gdn_body(q_ref, k_ref, v_ref, z_ref, wq_ref, wk_ref, wv_ref, acol_ref, bcol_ref, arow_ref,
              alog_ref, dtb_ref, nw_ref, o_ref, xq, xk, xv, state):
    ts = q_ref.shape[1]
    hb = q_ref.shape[2] // HEAD_DIM
    c = GDN_CHUNK
    nc = ts // c
    g = hb * nc

    @pl.when(pl.program_id(2) == 0)
    def _():
        for xp in (xq, xk, xv):
            xp[0:CONV_PAD, :] = jnp.zeros((CONV_PAD, hb * HEAD_DIM), F32)
        state[...] = jnp.zeros_like(state)

    def conv_act(x_ref, xp, w_ref, kind):
        xp[CONV_PAD:CONV_PAD + ts, :] = x_ref[0]
        w = w_ref[...]
        heads = []
        for h in range(hb):
            sl = slice(h * HEAD_DIM, (h + 1) * HEAD_DIM)
            y = None
            for k in range(CONV_W):
                tap = xp[pl.ds(CONV_PAD - (CONV_W - 1) + k, ts), sl] * w[k:k + 1, sl]
                y = tap if y is None else y + tap
            y = y * _sigmoid(y)
            if kind != "v":
                y = y * lax.rsqrt(jnp.sum(y * y, axis=-1, keepdims=True) + RMS_EPS)
            if kind == "q":
                y = y * (HEAD_DIM ** -0.5)
            heads.append(y.reshape(nc, c, HEAD_DIM))
        xp[0:CONV_PAD, :] = xp[ts:ts + CONV_PAD, :]
        return jnp.concatenate(heads, axis=0)

    qc = conv_act(q_ref, xq, wq_ref, "q")
    kc = conv_act(k_ref, xk, wk_ref, "k")
    vc = conv_act(v_ref, xv, wv_ref, "v")

    neg_a = -jnp.exp(alog_ref[...])
    dtb = dtb_ref[...]
    gcol = (neg_a * _softplus(acol_ref[0] + dtb)).reshape(g, c, 1)
    beta = _sigmoid(bcol_ref[0]).reshape(g, c, 1)
    grow = (neg_a[:, None] * _softplus(arow_ref[0] + dtb[:, None])).reshape(g, 1, c)

    ri = lax.broadcasted_iota(jnp.int32, (c, c), 0)
    ci = lax.broadcasted_iota(jnp.int32, (c, c), 1)
    lower = ri >= ci
    gc_col = jnp.sum(jnp.where(lower, grow, 0.0), axis=2, keepdims=True)
    gc_row = jnp.sum(jnp.where(ri <= ci, gcol, 0.0), axis=1, keepdims=True)
    gamma = jnp.where(lower, jnp.exp(jnp.where(lower, gc_col - gc_row, 0.0)), 0.0)
    kb = kc * beta
    p = jnp.where(ri > ci, -(_bmm_nt(kb, kc) * gamma), 0.0)
    r = p
    for _ in range(int(math.log2(c)) - 1):
        p = _bmm(p, p)
        r = r + p + _bmm(r, p)
    eg = jnp.exp(gc_col)
    vb = vc * beta
    kbe = kb * eg
    u = vb + _bmm(r, vb)
    w = kbe + _bmm(r, kbe)
    a_qk = _bmm_nt(qc, kc) * gamma
    g_last = gc_row[:, :, c - 1:c]
    k_tail = kc * jnp.exp(g_last - gc_col)
    q_dec = qc * eg
    decay = jnp.exp(g_last)

    per_head = lambda x: x.reshape((hb, nc) + x.shape[1:])
    u, w, a_qk, k_tail, q_dec, decay = [per_head(x) for x in (u, w, a_qk, k_tail, q_dec, decay)]
    nw = nw_ref[...]
    st = state[...]
    for i in range(nc):
        v_new = u[:, i] - _bmm(w[:, i], st)
        o = _bmm(q_dec[:, i], st) + _bmm(a_qk[:, i], v_new)
        kt = jnp.swapaxes(k_tail[:, i], 1, 2)
        st = st * decay[:, i] + _bmm(kt, v_new)
        o = o * lax.rsqrt(jnp.mean(o * o, axis=-1, keepdims=True) + RMS_EPS) * nw
        for h in range(hb):
            rows = slice(i * c, (i + 1) * c)
            cols = slice(h * HEAD_DIM, (h + 1) * HEAD_DIM)
            zz = z_ref[0, rows, cols]
            o_ref[0, rows, cols] = (o[h] * (zz * _sigmoid(zz))).astype(o_ref.dtype)
    state[...] = st


def _gdn(qkvz, ab, conv_w, a_log, dt_bias, norm_w, heads, heads_per_step=4, ts=512):
    b, s, _ = qkvz.shape
    c = GDN_CHUNK
    hb = min(heads_per_step, heads)
    ts = min(ts, s)
    ng = heads // hb
    a_t = jnp.transpose(ab[:, :, :heads], (0, 2, 1))
    b_t = jnp.transpose(ab[:, :, heads:2 * heads], (0, 2, 1))
    acol = a_t.reshape(b, heads, s, 1)
    bcol = b_t.reshape(b, heads, s, 1)
    arow = a_t.reshape(b, heads, s // c, 1, c)
    wide = hb * HEAD_DIM
    head = lambda off: pl.BlockSpec((1, ts, wide), lambda i, j, t: (i, t, off + j))
    cw = lambda off: pl.BlockSpec((CONV_W, wide), lambda i, j, t: (0, off + j))
    col = pl.BlockSpec((1, hb, ts, 1), lambda i, j, t: (i, j, t, 0))
    scal = pl.BlockSpec((hb, 1, 1), lambda i, j, t: (j, 0, 0))
    return pl.pallas_call(
        _gdn_body,
        grid=(b, ng, s // ts),
        in_specs=[head(0), head(ng), head(2 * ng), head(3 * ng),
                  cw(0), cw(ng), cw(2 * ng), col, col,
                  pl.BlockSpec((1, hb, ts // c, 1, c), lambda i, j, t: (i, j, t, 0, 0)),
                  scal, scal, pl.BlockSpec((1, HEAD_DIM), lambda i, j, t: (0, 0))],
        out_specs=pl.BlockSpec((1, ts, wide), lambda i, j, t: (i, t, j)),
        out_shape=jax.ShapeDtypeStruct((b, s, heads * HEAD_DIM), BF16),
        scratch_shapes=[pltpu.VMEM((ts + CONV_PAD, wide), F32)] * 3 + [pltpu.VMEM((hb, HEAD_DIM, HEAD_DIM), F32)],
        compiler_params=_params(("parallel", "parallel", "arbitrary")),
    )(qkvz, qkvz, qkvz, qkvz, conv_w, conv_w, conv_w, acol, bcol, arow,
      a_log.reshape(heads, 1, 1), dt_bias.reshape(heads, 1, 1), norm_w.reshape(1, HEAD_DIM))


def _shift_rows(x, d, fill):
    row = lax.broadcasted_iota(jnp.int32, x.shape, 0)
    return jnp.where(row >= d, pltpu.roll(x, d, 0), fill)


def _rg_body(x_ref, gate_ref, cw_ref, cb_ref, wa_ref, wx_ref, ba_ref, bx_ref, lam_ref, o_ref,
             xpad, a_s, h_s):
    s = x_ref.shape[1]
    cb = x_ref.shape[2]
    nb = cb // RG_BLOCK
    rt = _row_tile(s)
    xpad[0:CONV_PAD, :] = jnp.zeros((CONV_PAD, cb), F32)
    xpad[CONV_PAD:CONV_PAD + s, :] = x_ref[0]
    w = cw_ref[...]
    bias = cb_ref[...]
    lam_sp = _softplus(-lam_ref[...])
    for r in range(0, s, rt):
        xr = _conv_rows(xpad, w, r, rt) + bias
        for n in range(nb):
            sl = slice(n * RG_BLOCK, (n + 1) * RG_BLOCK)
            xb = xr[:, sl]
            xb16 = xb.astype(BF16)
            rgate = _sigmoid(_dot(xb16, wa_ref[n]) + ba_ref[:, sl])
            igate = _sigmoid(_dot(xb16, wx_ref[n]) + bx_ref[:, sl])
            log_a = -LRU_C * rgate * lam_sp[:, sl]
            a = jnp.exp(log_a)
            a_s[r:r + rt, sl] = a
            h_s[r:r + rt, sl] = jnp.sqrt(jnp.maximum(1.0 - a * a, 0.0)) * (igate * xb)

    def step(i, carry):
        r0 = pl.multiple_of(i * SUBLANES, SUBLANES)
        a = a_s[pl.ds(r0, SUBLANES), :]
        bv = h_s[pl.ds(r0, SUBLANES), :]
        for d in (1, 2, 4):
            bv = a * _shift_rows(bv, d, 0.0) + bv
            a = a * _shift_rows(a, d, 1.0)
        h = bv + a * carry
        h_s[pl.ds(r0, SUBLANES), :] = h
        return h[SUBLANES - 1:SUBLANES, :]

    lax.fori_loop(0, s // SUBLANES, step, jnp.zeros((1, cb), F32))
    for r in range(0, s, rt):
        o_ref[0, r:r + rt, :] = (h_s[r:r + rt, :] * _gelu(gate_ref[0, r:r + rt, :])).astype(o_ref.dtype)


def _rg_lru(rg, conv_w, conv_b, wa, wx, ba, bx, lam, blocks_per_step=4):
    b, s, w2 = rg.shape
    width = w2 // 2
    nblk = width // RG_BLOCK
    nb = min(blocks_per_step, nblk)
    cb = nb * RG_BLOCK
    steps = width // cb
    vec = pl.BlockSpec((1, cb), lambda i, j: (0, j))
    wblk = pl.BlockSpec((nb, RG_BLOCK, RG_BLOCK), lambda i, j: (j, 0, 0))
    return pl.pallas_call(
        _rg_body,
        grid=(b, steps),
        in_specs=[pl.BlockSpec((1, s, cb), lambda i, j: (i, 0, j)),
                  pl.BlockSpec((1, s, cb), lambda i, j: (i, 0, steps + j)),
                  pl.BlockSpec((CONV_W, cb), lambda i, j: (0, j)), vec, wblk, wblk, vec, vec, vec],
        out_specs=pl.BlockSpec((1, s, cb), lambda i, j: (i, 0, j)),
        out_shape=jax.ShapeDtypeStruct((b, s, width), BF16),
        scratch_shapes=[pltpu.VMEM((s + CONV_PAD, cb), F32), pltpu.VMEM((s, cb), F32), pltpu.VMEM((s, cb), F32)],
        compiler_params=_params(("parallel", "parallel")),
    )(rg, rg, conv_w, conv_b.reshape(1, width), wa.astype(BF16), wx.astype(BF16),
      ba.reshape(1, width), bx.reshape(1, width), lam.reshape(1, width))


SB_LOG_WEIGHT_FLOOR = -104.0


def _sb_body(q_ref, k_ref, v_ref, o_ref, *, bk):
    bq = q_ref.shape[1]
    hb = q_ref.shape[2] // HEAD_DIM
    qi = pl.program_id(2)
    t_pos = qi * bq + lax.broadcasted_iota(jnp.int32, (bq, bk), 0)
    s_off = lax.broadcasted_iota(jnp.int32, (bq, bk), 1)
    tri = (lax.broadcasted_iota(jnp.int32, (bk, bk), 0) > lax.broadcasted_iota(jnp.int32, (bk, bk), 1)).astype(BF16)
    nkb = (qi + 1) * (bq // bk)
    cols = [slice(h * HEAD_DIM, (h + 1) * HEAD_DIM) for h in range(hb)]
    q16 = [(q_ref[0, :, sl] * (HEAD_DIM ** -0.5)).astype(BF16) for sl in cols]

    def cond(carry):
        i, alive = carry[0], carry[1]
        return jnp.logical_and(i < nkb, alive)

    def body(carry):
        i, _, accs, runs = carry
        r0 = pl.multiple_of((nkb - 1 - i) * bk, bk)
        valid = (s_off + r0) < t_pos
        new_accs, new_runs = [], []
        top = None
        for h in range(hb):
            kblk = k_ref[0, pl.ds(r0, bk), cols[h]].astype(BF16)
            vblk = v_ref[0, pl.ds(r0, bk), cols[h]].astype(BF16)
            z = _dot_nt(q16[h], kblk)
            sp = _softplus(z)
            lm = jnp.where(valid, -sp, 0.0)
            hi = lm.astype(BF16)
            lo = (lm - hi.astype(F32)).astype(BF16)
            tail = _dot(hi, tri) + _dot(lo, tri) + runs[h]
            wts = jnp.where(valid, jnp.exp(z - sp + tail), 0.0)
            new_accs.append(accs[h] + _dot(wts.astype(BF16), vblk))
            run = runs[h] + jnp.sum(lm, axis=1, keepdims=True)
            new_runs.append(run)
            m = jnp.max(run)
            top = m if top is None else jnp.maximum(top, m)
        return i + 1, top > SB_LOG_WEIGHT_FLOOR, tuple(new_accs), tuple(new_runs)

    init = (jnp.int32(0), jnp.bool_(True),
            tuple(jnp.zeros((bq, HEAD_DIM), F32) for _ in range(hb)),
            tuple(jnp.zeros((bq, 1), F32) for _ in range(hb)))
    accs = lax.while_loop(cond, body, init)[2]
    for h in range(hb):
        o_ref[0, :, cols[h]] = accs[h].astype(o_ref.dtype)


def _stick_breaking(proj, heads, bq=256, bk=128, heads_per_step=2):
    b, s, _ = proj.shape
    bq = min(bq, s)
    bk = min(bk, bq)
    hb = min(heads_per_step, heads)
    ng = heads // hb
    wide = hb * HEAD_DIM
    return pl.pallas_call(
        functools.partial(_sb_body, bk=bk),
        grid=(b, ng, s // bq),
        in_specs=[pl.BlockSpec((1, bq, wide), lambda i, j, t: (i, t, j)),
                  pl.BlockSpec((1, s, wide), lambda i, j, t: (i, 0, ng + j)),
                  pl.BlockSpec((1, s, wide), lambda i, j, t: (i, 0, 2 * ng + j))],
        out_specs=pl.BlockSpec((1, bq, wide), lambda i, j, t: (i, t, j)),
        out_shape=jax.ShapeDtypeStruct((b, s, heads * HEAD_DIM), BF16),
        compiler_params=_params(("parallel", "parallel", "arbitrary")),
    )(proj, proj, proj)


def _s5_prep_body(are_ref, aim_ref, ldt_ref, abre_ref, abim_ref, cr_ref, ci_ref):
    dt = jnp.exp(ldt_ref[...])
    lr = jnp.minimum(are_ref[...], -1e-4)
    li = aim_ref[...]
    mag = jnp.exp(lr * dt)
    ab_re = mag * jnp.cos(li * dt)
    ab_im = mag * jnp.sin(li * dt)
    den = lr * lr + li * li
    nr = ab_re - 1.0
    cr_ref[...] = (nr * lr + ab_im * li) / den
    ci_ref[...] = (ab_im * lr - nr * li) / den
    p_re, p_im = ab_re, ab_im
    for k in range(SUBLANES):
        abre_ref[k] = p_re
        abim_ref[k] = p_im
        p_re, p_im = p_re * ab_re - p_im * ab_im, p_re * ab_im + p_im * ab_re


def _s5_prep(a_re, a_im, log_dt):
    g, n = a_re.shape
    full = pl.BlockSpec((g, n), lambda: (0, 0))
    pw = pl.BlockSpec((SUBLANES, g, n), lambda: (0, 0, 0))
    return pl.pallas_call(
        _s5_prep_body,
        in_specs=[full, full, pl.BlockSpec((g, 1), lambda: (0, 0))],
        out_specs=[pw, pw, full, full],
        out_shape=[jax.ShapeDtypeStruct((SUBLANES, g, n), F32)] * 2 + [jax.ShapeDtypeStruct((g, n), F32)] * 2,
    )(a_re, a_im, log_dt.reshape(g, 1))


def _s5_body(u_ref, bre_ref, bim_ref, cre_ref, cim_ref, pre_ref, pim_ref, cr_ref, ci_ref, d_ref, o_ref,
             hre_s, him_s):
    s = u_ref.shape[1]
    rt = _row_tile(s)
    cr = cr_ref[...]
    ci = ci_ref[...]
    for r in range(0, s, rt):
        u = u_ref[0, r:r + rt, :]
        xr = _dot(u, bre_ref[0], HIGHEST)
        xi = _dot(u, bim_ref[0], HIGHEST)
        hre_s[r:r + rt, :] = cr * xr - ci * xi
        him_s[r:r + rt, :] = cr * xi + ci * xr
    p_re = pre_ref[...]
    p_im = pim_ref[...]

    def step(i, carry):
        c_re, c_im = carry
        r0 = pl.multiple_of(i * SUBLANES, SUBLANES)
        x_re = hre_s[pl.ds(r0, SUBLANES), :]
        x_im = him_s[pl.ds(r0, SUBLANES), :]
        for d in (1, 2, 4):
            a_re = p_re[d - 1:d, :]
            a_im = p_im[d - 1:d, :]
            s_re = _shift_rows(x_re, d, 0.0)
            s_im = _shift_rows(x_im, d, 0.0)
            x_re, x_im = x_re + a_re * s_re - a_im * s_im, x_im + a_re * s_im + a_im * s_re
        h_re = x_re + p_re * c_re - p_im * c_im
        h_im = x_im + p_re * c_im + p_im * c_re
        hre_s[pl.ds(r0, SUBLANES), :] = h_re
        him_s[pl.ds(r0, SUBLANES), :] = h_im
        return h_re[SUBLANES - 1:SUBLANES, :], h_im[SUBLANES - 1:SUBLANES, :]

    zero = jnp.zeros((1, hre_s.shape[1]), F32)
    lax.fori_loop(0, s // SUBLANES, step, (zero, zero))
    dvec = d_ref[...]
    for r in range(0, s, rt):
        y = _dot(hre_s[r:r + rt, :], cre_ref[0], HIGHEST) - _dot(him_s[r:r + rt, :], cim_ref[0], HIGHEST)
        y = y + dvec * u_ref[0, r:r + rt, :]
        o_ref[0, r:r + rt, :] = _gelu(y)


def _block_diag(x, nblk):
    g = x.shape[0] // nblk
    r, c = x.shape[1:]
    eye = jnp.eye(g, dtype=x.dtype)
    return jnp.einsum("jgrc,gh->jgrhc", x.reshape(nblk, g, r, c), eye).reshape(nblk, g * r, g * c)


def _s5(proj, col0, a_re, a_im, log_dt, b_re, b_im, c_re, c_im, dvec):
    b, s, _ = proj.shape
    g, n = a_re.shape
    width = g * S5_GROUP
    gpb = LANES // S5_GROUP
    nblk = g // gpb
    nst = gpb * n
    p_re, p_im, cr, ci = _s5_prep(a_re, a_im, log_dt)
    bmat_re = _block_diag(jnp.transpose(b_re, (0, 2, 1)), nblk)
    bmat_im = _block_diag(jnp.transpose(b_im, (0, 2, 1)), nblk)
    cmat_re = _block_diag(jnp.transpose(c_re, (0, 2, 1)), nblk)
    cmat_im = _block_diag(jnp.transpose(c_im, (0, 2, 1)), nblk)
    ublk = col0 // LANES
    st_row = lambda rows: pl.BlockSpec((rows, nst), lambda i, j: (0, j))
    return pl.pallas_call(
        _s5_body,
        grid=(b, nblk),
        in_specs=[pl.BlockSpec((1, s, LANES), lambda i, j: (i, 0, ublk + j)),
                  pl.BlockSpec((1, LANES, nst), lambda i, j: (j, 0, 0)),
                  pl.BlockSpec((1, LANES, nst), lambda i, j: (j, 0, 0)),
                  pl.BlockSpec((1, nst, LANES), lambda i, j: (j, 0, 0)),
                  pl.BlockSpec((1, nst, LANES), lambda i, j: (j, 0, 0)),
                  st_row(SUBLANES), st_row(SUBLANES), st_row(1), st_row(1),
                  pl.BlockSpec((1, LANES), lambda i, j: (0, j))],
        out_specs=pl.BlockSpec((1, s, LANES), lambda i, j: (i, 0, j)),
        out_shape=jax.ShapeDtypeStruct((b, s, width), F32),
        scratch_shapes=[pltpu.VMEM((s, nst), F32), pltpu.VMEM((s, nst), F32)],
        compiler_params=_params(("parallel", "parallel")),
    )(proj, bmat_re, bmat_im, cmat_re, cmat_im, p_re.reshape(SUBLANES, g * n), p_im.reshape(SUBLANES, g * n),
      cr.reshape(1, g * n), ci.reshape(1, g * n), dvec.reshape(1, width))


def _glu_body(y_ref, w_ref, b_ref, o_ref):
    y = y_ref[...]
    o_ref[...] = (y * _sigmoid(_dot(y.astype(BF16), w_ref[...]) + b_ref[...])).astype(o_ref.dtype)


def _glu(y, w, bias, tm=512):
    t, d = y.shape
    tm = min(tm, t)
    return pl.pallas_call(
        _glu_body,
        grid=(t // tm,),
        in_specs=[pl.BlockSpec((tm, d), lambda i: (i, 0)), pl.BlockSpec((d, d), lambda i: (0, 0)),
                  pl.BlockSpec((1, d), lambda i: (0, 0))],
        out_specs=pl.BlockSpec((tm, d), lambda i: (i, 0)),
        out_shape=jax.ShapeDtypeStruct((t, d), BF16),
        compiler_params=_params(("parallel",)),
    )(y, w.astype(BF16), bias.reshape(1, d))


def _top_rows(s, k):
    rows = []
    for _ in range(k):
        m = jnp.max(s, axis=0, keepdims=True)
        rows.append(m)
        s = jnp.where(s == m, NEG_BIG, s)
    return rows


def _peer_route_body(q_ref, keys_ref, d1_ref, e1_ref, s2_ref, e2_ref):
    half = keys_ref.shape[3]
    q = q_ref[...]
    s1 = _dot_nt(keys_ref[0, 0], q[:, :half], HIGHEST)
    s2 = _dot_nt(keys_ref[0, 1], q[:, half:], HIGHEST)
    nk = PEER_TOPK + 1
    v1 = _top_rows(s1, nk)
    pad = jnp.full((-nk % SUBLANES, q.shape[0]), NEG_BIG, F32)
    v2 = jnp.concatenate(_top_rows(s2, nk) + [pad], axis=0)
    cand = jnp.concatenate([r + v2 for r in v1], axis=0)
    top = _top_rows(cand, nk)
    zsum = jnp.ones_like(top[0])
    for r in top[1:PEER_TOPK]:
        zsum = zsum + jnp.exp(r - top[0])
    thr = 0.5 * (top[PEER_TOPK - 1] + top[PEER_TOPK])
    d1_ref[0] = thr - s1
    e1_ref[0] = jnp.exp(s1 - v1[0])
    s2_ref[0] = s2
    e2_ref[0] = jnp.exp(s2 - v2[0:1, :]) / zsum


def _peer_route(q, keys, tt=512):
    t = q.shape[0]
    heads, _, nkeys, half = keys.shape
    tt = min(tt, t)
    table = pl.BlockSpec((1, nkeys, tt), lambda i, h: (h, 0, i))
    shape = jax.ShapeDtypeStruct((heads, nkeys, t), F32)
    return pl.pallas_call(
        _peer_route_body,
        grid=(t // tt, heads),
        in_specs=[pl.BlockSpec((tt, 2 * half), lambda i, h: (i, h)),
                  pl.BlockSpec((1, 2, nkeys, half), lambda i, h: (h, 0, 0, 0))],
        out_specs=[table] * 4,
        out_shape=[shape] * 4,
        compiler_params=_params(("parallel", "parallel")),
    )(q, keys)


def _peer_dense_body(x_ref, u_ref, v_ref, d1_ref, e1_ref, s2_ref, e2_ref, o_ref):
    heads, nkeys, _ = s2_ref.shape
    te = u_ref.shape[0]
    j = pl.program_id(1)

    @pl.when(j == 0)
    def _():
        o_ref[...] = jnp.zeros_like(o_ref)

    act = _gelu(_dot_nt(u_ref[...], x_ref[...]))
    parts = []
    for r in range(te // nkeys):
        i1 = j * (te // nkeys) + r
        w = None
        for h in range(heads):
            d1 = d1_ref[h, pl.ds(i1, 1), :]
            e1 = e1_ref[h, pl.ds(i1, 1), :]
            term = jnp.where(s2_ref[h] >= d1, e2_ref[h] * e1, 0.0)
            w = term if w is None else w + term
        parts.append(w * act[r * nkeys:(r + 1) * nkeys, :])
    p_t = jnp.concatenate(parts, axis=0) if len(parts) > 1 else parts[0]
    o_ref[...] += _dot(p_t.T.astype(BF16), v_ref[...])


def _peer_dense(x16, u16, v16, tables, tt=512, te=512):
    t, d = x16.shape
    e = u16.shape[0]
    heads, nkeys, _ = tables[0].shape
    tt = min(tt, t)
    te = min(te, e)
    assert te % nkeys == 0 and e == nkeys * nkeys
    once = pl.Buffered(1)
    table = pl.BlockSpec((heads, nkeys, tt), lambda i, j: (0, 0, i), pipeline_mode=once)
    return pl.pallas_call(
        _peer_dense_body,
        grid=(t // tt, e // te),
        in_specs=[pl.BlockSpec((tt, d), lambda i, j: (i, 0), pipeline_mode=once),
                  pl.BlockSpec((te, d), lambda i, j: (j, 0)),
                  pl.BlockSpec((te, d), lambda i, j: (j, 0)),
                  table, table, table, table],
        out_specs=pl.BlockSpec((tt, d), lambda i, j: (i, 0)),
        out_shape=jax.ShapeDtypeStruct((t, d), F32),
        compiler_params=_params(("parallel", "arbitrary")),
    )(x16, u16, v16, *tables)


def _peer_ffn(h16, wq16, keys, u16, v16):
    q = _matmul(h16, wq16)
    return _peer_dense(h16, u16, v16, _peer_route(q, keys))


def _even_mixer(x2d, b, s, w_in, gdn_conv_w, a_log, dt_bias, norm_w, rg_conv_w, rg_conv_b,
                rg_wa, rg_ba, rg_wx, rg_bx, rg_lambda, w_out):
    heads = a_log.shape[0]
    gw = heads * HEAD_DIM
    rw = rg_lambda.shape[0]
    ab_pad = LANES - 2 * heads
    w_qkvz = w_in[:, :4 * gw].astype(BF16)
    w_ab = jnp.pad(w_in[:, 4 * gw:4 * gw + 2 * heads], ((0, 0), (0, ab_pad))).astype(BF16)
    w_rg = w_in[:, 4 * gw + 2 * heads:].astype(BF16)
    qkvz = _matmul(x2d, w_qkvz).reshape(b, s, 4 * gw)
    ab = _matmul(x2d, w_ab).reshape(b, s, LANES)
    rg = _matmul(x2d, w_rg).reshape(b, s, 2 * rw)
    gdn_out = _gdn(qkvz, ab, gdn_conv_w, a_log, dt_bias, norm_w, heads)
    rg_out = _rg_lru(rg, rg_conv_w, rg_conv_b, rg_wa, rg_wx, rg_ba, rg_bx, rg_lambda)
    mixed = jnp.concatenate([gdn_out, rg_out], axis=-1).reshape(b * s, gw + rw)
    return _matmul(mixed, w_out.astype(BF16))


def _odd_mixer(x2d, b, s, w_in, a_re, a_im, log_dt, b_re, b_im, c_re, c_im, dvec, glu_w, glu_b, w_out):
    sw = dvec.shape[0]
    sbw = (w_in.shape[1] - sw) // 3
    heads = sbw // HEAD_DIM
    proj = _matmul(x2d, w_in.astype(BF16)).reshape(b, s, 3 * sbw + sw)
    sb_out = _stick_breaking(proj, heads)
    yg = _s5(proj, 3 * sbw, a_re, a_im, log_dt, b_re, b_im, c_re, c_im, dvec)
    s5_out = _glu(yg.reshape(b * s, sw), glu_w, glu_b)
    mixed = jnp.concatenate([sb_out.reshape(b * s, sbw), s5_out], axis=-1)
    return _matmul(mixed, w_out.astype(BF16))


def kernel(x, w_in_e, gdn_conv_w, gdn_A_log, gdn_dt_bias, gdn_norm_w, rg_conv_w, rg_conv_b, rg_wa, rg_ba,
           rg_wx, rg_bx, rg_lambda, w_out_e, w_in_o, s5_A_re, s5_A_im, s5_log_dt, s5_B_re, s5_B_im,
           s5_C_re, s5_C_im, s5_D, s5_glu_w, s5_glu_b, w_out_o, ln_mix_g, ln_mix_b, peer_wq, peer_keys,
           peer_u, peer_v, ln_ffn_g, ln_ffn_b):
    b, s, d = x.shape
    depth = ln_mix_g.shape[0]
    h = x.reshape(b * s, d)
    h16 = h.astype(BF16)
    for layer in range(depth):
        j = layer // 2
        if layer % 2 == 0:
            mix = _even_mixer(h16, b, s, w_in_e[j], gdn_conv_w[j], gdn_A_log[j], gdn_dt_bias[j], gdn_norm_w[j],
                              rg_conv_w[j], rg_conv_b[j], rg_wa[j], rg_ba[j], rg_wx[j], rg_bx[j],
                              rg_lambda[j], w_out_e[j])
        else:
            mix = _odd_mixer(h16, b, s, w_in_o[j], s5_A_re[j], s5_A_im[j], s5_log_dt[j], s5_B_re[j],
                             s5_B_im[j], s5_C_re[j], s5_C_im[j], s5_D[j], s5_glu_w[j], s5_glu_b[j], w_out_o[j])
        h, h16 = _residual_ln(h, mix, ln_mix_g[layer], ln_mix_b[layer])
        ffn = _peer_ffn(h16, peer_wq[layer].astype(BF16), peer_keys[layer],
                        peer_u[layer].astype(BF16), peer_v[layer].astype(BF16))
        h, h16 = _residual_ln(h, ffn, ln_ffn_g[layer], ln_ffn_b[layer])
    return h.reshape(b, s, d)
```

```python
import functools
import math

import jax
import jax.numpy as jnp
from jax import lax
from jax.experimental import pallas as pl
from jax.experimental.pallas import tpu as pltpu

F32 = jnp.float32
BF16 = jnp.bfloat16
HIGHEST = lax.Precision.HIGHEST

LANES = 128
SUBLANES = 8
BF16_ROWS = 2 * SUBLANES
VMEM_BYTES_V7X = 64 * 1024 * 1024
VMEM_LIMIT = VMEM_BYTES_V7X - 8 * 1024 * 1024

HEAD_DIM = 128
CONV_W = 4
CONV_PAD = SUBLANES
GDN_CHUNK = 64
RG_BLOCK = 128
LRU_C = 8.0
S5_GROUP = 16
S5_STATE = 64
PEER_TOPK = 16
DEPTH = 2
DN_ALPHA = (2.0 * DEPTH) ** 0.25
LN_EPS = 1e-5
RMS_EPS = 1e-6
NEG_BIG = -3.0e38


def _params(sem, vmem=VMEM_LIMIT):
    return pltpu.CompilerParams(dimension_semantics=sem, vmem_limit_bytes=vmem)


def _softplus(x):
    return jnp.maximum(x, 0.0) + jnp.log1p(jnp.exp(-jnp.abs(x)))


def _sigmoid(x):
    return 1.0 / (1.0 + jnp.exp(-x))


def _gelu(x):
    return 0.5 * x * (1.0 + jnp.tanh(0.7978845608028654 * (x + 0.044715 * (x * x * x))))


def _dot(a, b, precision=None):
    return jnp.dot(a, b, preferred_element_type=F32, precision=precision)


def _dot_nt(a, b, precision=None):
    return lax.dot_general(a, b, (((1,), (1,)), ((), ())), preferred_element_type=F32, precision=precision)


def _dot_tn(a, b, precision=None):
    return lax.dot_general(a, b, (((0,), (0,)), ((), ())), preferred_element_type=F32, precision=precision)


def _mm_body(a_ref, b_ref, o_ref, acc_ref):
    k = pl.program_id(2)

    @pl.when(k == 0)
    def _():
        acc_ref[...] = jnp.zeros_like(acc_ref)

    acc_ref[...] += _dot(a_ref[...].astype(BF16), b_ref[...])

    @pl.when(k == pl.num_programs(2) - 1)
    def _():
        o_ref[...] = acc_ref[...].astype(o_ref.dtype)


def _mm_single_body(a_ref, b_ref, o_ref):
    o_ref[...] = _dot(a_ref[...].astype(BF16), b_ref[...]).astype(o_ref.dtype)


def _matmul(a, b, out_dtype=F32, tm=1024, tn=1024, tk=1024):
    m, kd = a.shape
    _, n = b.shape
    tm, tn = min(tm, m), min(tn, n)
    assert m % tm == 0 and n % tn == 0, (a.shape, b.shape)
    out_bytes = jnp.dtype(out_dtype).itemsize
    tile_bytes = 2 * (tm * kd * a.dtype.itemsize + kd * tn * b.dtype.itemsize + tm * tn * out_bytes)
    if tile_bytes <= (VMEM_LIMIT * 3) // 4:
        return pl.pallas_call(
            _mm_single_body,
            grid=(m // tm, n // tn),
            in_specs=[pl.BlockSpec((tm, kd), lambda i, j: (i, 0)),
                      pl.BlockSpec((kd, tn), lambda i, j: (0, j))],
            out_specs=pl.BlockSpec((tm, tn), lambda i, j: (i, j)),
            out_shape=jax.ShapeDtypeStruct((m, n), out_dtype),
            compiler_params=_params(("parallel", "parallel")),
        )(a, b)
    tk = min(tk, kd)
    assert kd % tk == 0, (a.shape, b.shape)
    return pl.pallas_call(
        _mm_body,
        grid=(m // tm, n // tn, kd // tk),
        in_specs=[pl.BlockSpec((tm, tk), lambda i, j, k: (i, k)),
                  pl.BlockSpec((tk, tn), lambda i, j, k: (k, j))],
        out_specs=pl.BlockSpec((tm, tn), lambda i, j, k: (i, j)),
        out_shape=jax.ShapeDtypeStruct((m, n), out_dtype),
        scratch_shapes=[pltpu.VMEM((tm, tn), F32)],
        compiler_params=_params(("parallel", "parallel", "arbitrary")),
    )(a, b)


def _mm_pair_body(a1_ref, a2_ref, b1_ref, b2_ref, o_ref):
    o_ref[...] = (_dot(a1_ref[...], b1_ref[...]) + _dot(a2_ref[...], b2_ref[...])).astype(o_ref.dtype)


def _matmul_pair(a1, a2, b, out_dtype=F32, tm=1024, tn=1024):
    m, k1 = a1.shape
    _, k2 = a2.shape
    _, n = b.shape
    tm, tn = min(tm, m), min(tn, n)
    assert m % tm == 0 and n % tn == 0 and k1 % k2 == 0 and b.shape[0] == k1 + k2
    return pl.pallas_call(
        _mm_pair_body,
        grid=(m // tm, n // tn),
        in_specs=[pl.BlockSpec((tm, k1), lambda i, j: (i, 0)),
                  pl.BlockSpec((tm, k2), lambda i, j: (i, 0)),
                  pl.BlockSpec((k1, tn), lambda i, j: (0, j)),
                  pl.BlockSpec((k2, tn), lambda i, j: (k1 // k2, j))],
        out_specs=pl.BlockSpec((tm, tn), lambda i, j: (i, j)),
        out_shape=jax.ShapeDtypeStruct((m, n), out_dtype),
        compiler_params=_params(("parallel", "parallel")),
    )(a1, a2, b, b)


def _cast_body(x_ref, o_ref):
    o_ref[...] = x_ref[0].astype(o_ref.dtype)


def _stage_bf16(x, layer, cols=None, tile_bytes=8 * 1024 * 1024):
    _, r, c = x.shape
    cols = c if cols is None else cols
    assert cols % LANES == 0
    tr = min(r, 1 << int(math.log2(tile_bytes // (cols * x.dtype.itemsize))))
    assert r % tr == 0 and tr % BF16_ROWS == 0
    return pl.pallas_call(
        _cast_body,
        grid=(r // tr,),
        in_specs=[pl.BlockSpec((1, tr, cols), lambda i: (layer, i, 0))],
        out_specs=pl.BlockSpec((tr, cols), lambda i: (i, 0)),
        out_shape=jax.ShapeDtypeStruct((r, cols), BF16),
        compiler_params=_params(("parallel",)),
    )(x)


def _ln_body(h_ref, d_ref, g_ref, b_ref, o_ref, ob_ref):
    y = DN_ALPHA * h_ref[...] + d_ref[...]
    mu = jnp.mean(y, axis=-1, keepdims=True)
    yc = y - mu
    var = jnp.mean(yc * yc, axis=-1, keepdims=True)
    out = yc * lax.rsqrt(var + LN_EPS) * g_ref[...] + b_ref[...]
    o_ref[...] = out
    ob_ref[...] = out.astype(BF16)


def _residual_ln(h, delta, g, b, tm=256):
    t, d = h.shape
    tm = min(tm, t)
    row = pl.BlockSpec((tm, d), lambda i: (i, 0))
    vec = pl.BlockSpec((1, d), lambda i: (0, 0))
    return pl.pallas_call(
        _ln_body,
        grid=(t // tm,),
        in_specs=[row, row, vec, vec],
        out_specs=[row, row],
        out_shape=[jax.ShapeDtypeStruct((t, d), F32), jax.ShapeDtypeStruct((t, d), BF16)],
        compiler_params=_params(("parallel",)),
    )(h, delta, g.reshape(1, d), b.reshape(1, d))


def _conv_rows(xpad, w, r, rows):
    acc = None
    for k in range(CONV_W):
        tap = xpad[pl.ds(r + CONV_PAD - (CONV_W - 1) + k, rows), :] * w[k:k + 1, :]
        acc = tap if acc is None else acc + tap
    return acc


def _row_tile(s):
    return min(s, 256)


def _bmm(a, b):
    return jnp.einsum("gmk,gkn->gmn", a.astype(BF16), b.astype(BF16), preferred_element_type=F32)


def _bmm_nt(a, b):
    return jnp.einsum("gmk,gnk->gmn", a.astype(BF16), b.astype(BF16), preferred_element_type=F32)


def _gdn_body(q_ref, k_ref, v_ref, z_ref, wq_ref, wk_ref, wv_ref, acol_ref, bcol_ref, arow_ref,
              alog_ref, dtb_ref, nw_ref, o_ref, xq, xk, xv, state):
    ts = q_ref.shape[1]
    hb = q_ref.shape[2] // HEAD_DIM
    c = GDN_CHUNK
    nc = ts // c
    g = hb * nc

    @pl.when(pl.program_id(2) == 0)
    def _():
        for xp in (xq, xk, xv):
            xp[0:CONV_PAD, :] = jnp.zeros((CONV_PAD, hb * HEAD_DIM), F32)
        state[...] = jnp.zeros_like(state)

    def conv_act(x_ref, xp, w_ref, kind):
        xp[CONV_PAD:CONV_PAD + ts, :] = x_ref[0]
        w = w_ref[...]
        heads = []
        for h in range(hb):
            sl = slice(h * HEAD_DIM, (h + 1) * HEAD_DIM)
            y = None
            for k in range(CONV_W):
                tap = xp[pl.ds(CONV_PAD - (CONV_W - 1) + k, ts), sl] * w[k:k + 1, sl]
                y = tap if y is None else y + tap
            y = y * _sigmoid(y)
            if kind != "v":
                y = y * lax.rsqrt(jnp.sum(y * y, axis=-1, keepdims=True) + RMS_EPS)
            if kind == "q":
                y = y * (HEAD_DIM ** -0.5)
            heads.append(y.reshape(nc, c, HEAD_DIM))
        xp[0:CONV_PAD, :] = xp[ts:ts + CONV_PAD, :]
        return jnp.concatenate(heads, axis=0)

    qc = conv_act(q_ref, xq, wq_ref, "q")
    kc = conv_act(k_ref, xk, wk_ref, "k")
    vc = conv_act(v_ref, xv, wv_ref, "v")

    neg_a = -jnp.exp(alog_ref[...])
    dtb = dtb_ref[...]
    gcol = (neg_a * _softplus(acol_ref[0] + dtb)).reshape(g, c, 1)
    beta = _sigmoid(bcol_ref[0]).reshape(g, c, 1)
    grow = (neg_a[:, None] * _softplus(arow_ref[0] + dtb[:, None])).reshape(g, 1, c)

    ri = lax.broadcasted_iota(jnp.int32, (c, c), 0)
    ci = lax.broadcasted_iota(jnp.int32, (c, c), 1)
    lower = ri >= ci
    gc_col = jnp.sum(jnp.where(lower, grow, 0.0), axis=2, keepdims=True)
    gc_row = jnp.sum(jnp.where(ri <= ci, gcol, 0.0), axis=1, keepdims=True)
    gamma = jnp.where(lower, jnp.exp(jnp.where(lower, gc_col - gc_row, 0.0)), 0.0)
    kb = kc * beta
    p = jnp.where(ri > ci, -(_bmm_nt(kb, kc) * gamma), 0.0)
    r = p
    for _ in range(int(math.log2(c)) - 1):
        p = _bmm(p, p)
        r = r + p + _bmm(r, p)
    eg = jnp.exp(gc_col)
    vb = vc * beta
    kbe = kb * eg
    u = vb + _bmm(r, vb)
    w = kbe + _bmm(r, kbe)
    a_qk = _bmm_nt(qc, kc) * gamma
    g_last = gc_row[:, :, c - 1:c]
    k_tail = kc * jnp.exp(g_last - gc_col)
    q_dec = qc * eg
    decay = jnp.exp(g_last)

    per_head = lambda x: x.reshape((hb, nc) + x.shape[1:])
    u, w, a_qk, k_tail, q_dec, decay = [per_head(x) for x in (u, w, a_qk, k_tail, q_dec, decay)]
    nw = nw_ref[...]
    st = state[...]
    for i in range(nc):
        v_new = u[:, i] - _bmm(w[:, i], st)
        o = _bmm(q_dec[:, i], st) + _bmm(a_qk[:, i], v_new)
        kt = jnp.swapaxes(k_tail[:, i], 1, 2)
        st = st * decay[:, i] + _bmm(kt, v_new)
        o = o * lax.rsqrt(jnp.mean(o * o, axis=-1, keepdims=True) + RMS_EPS) * nw
        for h in range(hb):
            rows = slice(i * c, (i + 1) * c)
            cols = slice(h * HEAD_DIM, (h + 1) * HEAD_DIM)
            zz = z_ref[0, rows, cols]
            o_ref[0, rows, cols] = (o[h] * (zz * _sigmoid(zz))).astype(o_ref.dtype)
    state[...] = st


def _gdn(qkvz, ab, conv_w, a_log, dt_bias, norm_w, heads, heads_per_step=4, ts=512):
    b, s, _ = qkvz.shape
    c = GDN_CHUNK
    hb = min(heads_per_step, heads)
    ts = min(ts, s)
    ng = heads // hb
    a_t = jnp.transpose(ab[:, :, :heads], (0, 2, 1))
    b_t = jnp.transpose(ab[:, :, heads:2 * heads], (0, 2, 1))
    acol = a_t.reshape(b, heads, s, 1)
    bcol = b_t.reshape(b, heads, s, 1)
    arow = a_t.reshape(b, heads, s // c, 1, c)
    wide = hb * HEAD_DIM
    head = lambda off: pl.BlockSpec((1, ts, wide), lambda i, j, t: (i, t, off + j))
    cw = lambda off: pl.BlockSpec((CONV_W, wide), lambda i, j, t: (0, off + j))
    col = pl.BlockSpec((1, hb, ts, 1), lambda i, j, t: (i, j, t, 0))
    scal = pl.BlockSpec((hb, 1, 1), lambda i, j, t: (j, 0, 0))
    return pl.pallas_call(
        _gdn_body,
        grid=(b, ng, s // ts),
        in_specs=[head(0), head(ng), head(2 * ng), head(3 * ng),
                  cw(0), cw(ng), cw(2 * ng), col, col,
                  pl.BlockSpec((1, hb, ts // c, 1, c), lambda i, j, t: (i, j, t, 0, 0)),
                  scal, scal, pl.BlockSpec((1, HEAD_DIM), lambda i, j, t: (0, 0))],
        out_specs=pl.BlockSpec((1, ts, wide), lambda i, j, t: (i, t, j)),
        out_shape=jax.ShapeDtypeStruct((b, s, heads * HEAD_DIM), BF16),
        scratch_shapes=[pltpu.VMEM((ts + CONV_PAD, wide), F32)] * 3 + [pltpu.VMEM((hb, HEAD_DIM, HEAD_DIM), F32)],
        compiler_params=_params(("parallel", "parallel", "arbitrary")),
    )(qkvz, qkvz, qkvz, qkvz, conv_w, conv_w, conv_w, acol, bcol, arow,
      a_log.reshape(heads, 1, 1), dt_bias.reshape(heads, 1, 1), norm_w.reshape(1, HEAD_DIM))


def _shift_rows(x, d, fill):
    row = lax.broadcasted_iota(jnp.int32, x.shape, 0)
    return jnp.where(row >= d, pltpu.roll(x, d, 0), fill)


def _rg_body(x_ref, gate_ref, cw_ref, cb_ref, wa_ref, wx_ref, ba_ref, bx_ref, lam_ref, o_ref,
             xpad, a_s, h_s):
    s = x_ref.shape[1]
    cb = x_ref.shape[2]
    nb = cb // RG_BLOCK
    rt = _row_tile(s)
    xpad[0:CONV_PAD, :] = jnp.zeros((CONV_PAD, cb), F32)
    xpad[CONV_PAD:CONV_PAD + s, :] = x_ref[0]
    w = cw_ref[...]
    bias = cb_ref[...]
    lam_sp = _softplus(-lam_ref[...])
    for r in range(0, s, rt):
        xr = _conv_rows(xpad, w, r, rt) + bias
        for n in range(nb):
            sl = slice(n * RG_BLOCK, (n + 1) * RG_BLOCK)
            xb = xr[:, sl]
            xb16 = xb.astype(BF16)
            rgate = _sigmoid(_dot(xb16, wa_ref[n]) + ba_ref[:, sl])
            igate = _sigmoid(_dot(xb16, wx_ref[n]) + bx_ref[:, sl])
            log_a = -LRU_C * rgate * lam_sp[:, sl]
            a = jnp.exp(log_a)
            a_s[r:r + rt, sl] = a
            h_s[r:r + rt, sl] = jnp.sqrt(jnp.maximum(1.0 - a * a, 0.0)) * (igate * xb)

    def step(i, carry):
        r0 = pl.multiple_of(i * SUBLANES, SUBLANES)
        a = a_s[pl.ds(r0, SUBLANES), :]
        bv = h_s[pl.ds(r0, SUBLANES), :]
        for d in (1, 2, 4):
            bv = a * _shift_rows(bv, d, 0.0) + bv
            a = a * _shift_rows(a, d, 1.0)
        h = bv + a * carry
        h_s[pl.ds(r0, SUBLANES), :] = h
        return h[SUBLANES - 1:SUBLANES, :]

    lax.fori_loop(0, s // SUBLANES, step, jnp.zeros((1, cb), F32))
    for r in range(0, s, rt):
        o_ref[0, r:r + rt, :] = (h_s[r:r + rt, :] * _gelu(gate_ref[0, r:r + rt, :])).astype(o_ref.dtype)


def _rg_lru(rg, conv_w, conv_b, wa, wx, ba, bx, lam, blocks_per_step=4):
    b, s, w2 = rg.shape
    width = w2 // 2
    nblk = width // RG_BLOCK
    nb = min(blocks_per_step, nblk)
    cb = nb * RG_BLOCK
    steps = width // cb
    vec = pl.BlockSpec((1, cb), lambda i, j: (0, j))
    wblk = pl.BlockSpec((nb, RG_BLOCK, RG_BLOCK), lambda i, j: (j, 0, 0))
    return pl.pallas_call(
        _rg_body,
        grid=(b, steps),
        in_specs=[pl.BlockSpec((1, s, cb), lambda i, j: (i, 0, j)),
                  pl.BlockSpec((1, s, cb), lambda i, j: (i, 0, steps + j)),
                  pl.BlockSpec((CONV_W, cb), lambda i, j: (0, j)), vec, wblk, wblk, vec, vec, vec],
        out_specs=pl.BlockSpec((1, s, cb), lambda i, j: (i, 0, j)),
        out_shape=jax.ShapeDtypeStruct((b, s, width), BF16),
        scratch_shapes=[pltpu.VMEM((s + CONV_PAD, cb), F32), pltpu.VMEM((s, cb), F32), pltpu.VMEM((s, cb), F32)],
        compiler_params=_params(("parallel", "parallel")),
    )(rg, rg, conv_w, conv_b.reshape(1, width), wa.astype(BF16), wx.astype(BF16),
      ba.reshape(1, width), bx.reshape(1, width), lam.reshape(1, width))


SB_LOG_WEIGHT_FLOOR = -104.0


def _sb_body(q_ref, k_ref, v_ref, o_ref, *, bk):
    bq = q_ref.shape[1]
    hb = q_ref.shape[2] // HEAD_DIM
    qi = pl.program_id(2)
    t_pos = qi * bq + lax.broadcasted_iota(jnp.int32, (bq, bk), 0)
    s_off = lax.broadcasted_iota(jnp.int32, (bq, bk), 1)
    tri = (lax.broadcasted_iota(jnp.int32, (bk, bk), 0) > lax.broadcasted_iota(jnp.int32, (bk, bk), 1)).astype(BF16)
    nkb = (qi + 1) * (bq // bk)
    cols = [slice(h * HEAD_DIM, (h + 1) * HEAD_DIM) for h in range(hb)]
    q16 = [(q_ref[0, :, sl] * (HEAD_DIM ** -0.5)).astype(BF16) for sl in cols]

    def cond(carry):
        i, alive = carry[0], carry[1]
        return jnp.logical_and(i < nkb, alive)

    def body(carry):
        i, _, accs, runs = carry
        r0 = pl.multiple_of((nkb - 1 - i) * bk, bk)
        valid = (s_off + r0) < t_pos
        new_accs, new_runs = [], []
        top = None
        for h in range(hb):
            kblk = k_ref[0, pl.ds(r0, bk), cols[h]].astype(BF16)
            vblk = v_ref[0, pl.ds(r0, bk), cols[h]].astype(BF16)
            z = _dot_nt(q16[h], kblk)
            sp = _softplus(z)
            lm = jnp.where(valid, -sp, 0.0)
            hi = lm.astype(BF16)
            lo = (lm - hi.astype(F32)).astype(BF16)
            tail = _dot(hi, tri) + _dot(lo, tri) + runs[h]
            wts = jnp.where(valid, jnp.exp(z - sp + tail), 0.0)
            new_accs.append(accs[h] + _dot(wts.astype(BF16), vblk))
            run = runs[h] + jnp.sum(lm, axis=1, keepdims=True)
            new_runs.append(run)
            m = jnp.max(run)
            top = m if top is None else jnp.maximum(top, m)
        return i + 1, top > SB_LOG_WEIGHT_FLOOR, tuple(new_accs), tuple(new_runs)

    init = (jnp.int32(0), jnp.bool_(True),
            tuple(jnp.zeros((bq, HEAD_DIM), F32) for _ in range(hb)),
            tuple(jnp.zeros((bq, 1), F32) for _ in range(hb)))
    accs = lax.while_loop(cond, body, init)[2]
    for h in range(hb):
        o_ref[0, :, cols[h]] = accs[h].astype(o_ref.dtype)


def _stick_breaking(proj, heads, bq=256, bk=256, heads_per_step=2):
    b, s, _ = proj.shape
    bq = min(bq, s)
    bk = min(bk, bq)
    hb = min(heads_per_step, heads)
    ng = heads // hb
    wide = hb * HEAD_DIM
    return pl.pallas_call(
        functools.partial(_sb_body, bk=bk),
        grid=(b, ng, s // bq),
        in_specs=[pl.BlockSpec((1, bq, wide), lambda i, j, t: (i, t, j)),
                  pl.BlockSpec((1, s, wide), lambda i, j, t: (i, 0, ng + j)),
                  pl.BlockSpec((1, s, wide), lambda i, j, t: (i, 0, 2 * ng + j))],
        out_specs=pl.BlockSpec((1, bq, wide), lambda i, j, t: (i, t, j)),
        out_shape=jax.ShapeDtypeStruct((b, s, heads * HEAD_DIM), BF16),
        compiler_params=_params(("parallel", "parallel", "arbitrary")),
    )(proj, proj, proj)


def _s5_prep_body(are_ref, aim_ref, ldt_ref, abre_ref, abim_ref, cr_ref, ci_ref):
    dt = jnp.exp(ldt_ref[...])
    lr = jnp.minimum(are_ref[...], -1e-4)
    li = aim_ref[...]
    mag = jnp.exp(lr * dt)
    ab_re = mag * jnp.cos(li * dt)
    ab_im = mag * jnp.sin(li * dt)
    den = lr * lr + li * li
    nr = ab_re - 1.0
    cr_ref[...] = (nr * lr + ab_im * li) / den
    ci_ref[...] = (ab_im * lr - nr * li) / den
    p_re, p_im = ab_re, ab_im
    for k in range(SUBLANES):
        abre_ref[k] = p_re
        abim_ref[k] = p_im
        p_re, p_im = p_re * ab_re - p_im * ab_im, p_re * ab_im + p_im * ab_re


def _s5_prep(a_re, a_im, log_dt):
    g, n = a_re.shape
    full = pl.BlockSpec((g, n), lambda: (0, 0))
    pw = pl.BlockSpec((SUBLANES, g, n), lambda: (0, 0, 0))
    return pl.pallas_call(
        _s5_prep_body,
        in_specs=[full, full, pl.BlockSpec((g, 1), lambda: (0, 0))],
        out_specs=[pw, pw, full, full],
        out_shape=[jax.ShapeDtypeStruct((SUBLANES, g, n), F32)] * 2 + [jax.ShapeDtypeStruct((g, n), F32)] * 2,
    )(a_re, a_im, log_dt.reshape(g, 1))


def _s5_body(u_ref, bre_ref, bim_ref, cre_ref, cim_ref, pre_ref, pim_ref, cr_ref, ci_ref, d_ref, o_ref,
             hre_s, him_s):
    s = u_ref.shape[1]
    rt = _row_tile(s)
    cr = cr_ref[...]
    ci = ci_ref[...]
    for r in range(0, s, rt):
        u = u_ref[0, r:r + rt, :]
        xr = _dot(u, bre_ref[0], HIGHEST)
        xi = _dot(u, bim_ref[0], HIGHEST)
        hre_s[r:r + rt, :] = cr * xr - ci * xi
        him_s[r:r + rt, :] = cr * xi + ci * xr
    p_re = pre_ref[...]
    p_im = pim_ref[...]

    def step(i, carry):
        c_re, c_im = carry
        r0 = pl.multiple_of(i * SUBLANES, SUBLANES)
        x_re = hre_s[pl.ds(r0, SUBLANES), :]
        x_im = him_s[pl.ds(r0, SUBLANES), :]
        for d in (1, 2, 4):
            a_re = p_re[d - 1:d, :]
            a_im = p_im[d - 1:d, :]
            s_re = _shift_rows(x_re, d, 0.0)
            s_im = _shift_rows(x_im, d, 0.0)
            x_re, x_im = x_re + a_re * s_re - a_im * s_im, x_im + a_re * s_im + a_im * s_re
        h_re = x_re + p_re * c_re - p_im * c_im
        h_im = x_im + p_re * c_im + p_im * c_re
        hre_s[pl.ds(r0, SUBLANES), :] = h_re
        him_s[pl.ds(r0, SUBLANES), :] = h_im
        return h_re[SUBLANES - 1:SUBLANES, :], h_im[SUBLANES - 1:SUBLANES, :]

    zero = jnp.zeros((1, hre_s.shape[1]), F32)
    lax.fori_loop(0, s // SUBLANES, step, (zero, zero))
    dvec = d_ref[...]
    for r in range(0, s, rt):
        y = _dot(hre_s[r:r + rt, :], cre_ref[0], HIGHEST) - _dot(him_s[r:r + rt, :], cim_ref[0], HIGHEST)
        y = y + dvec * u_ref[0, r:r + rt, :]
        o_ref[0, r:r + rt, :] = _gelu(y)


def _block_diag(x, nblk):
    g = x.shape[0] // nblk
    r, c = x.shape[1:]
    eye = jnp.eye(g, dtype=x.dtype)
    return jnp.einsum("jgrc,gh->jgrhc", x.reshape(nblk, g, r, c), eye).reshape(nblk, g * r, g * c)


def _s5(proj, col0, a_re, a_im, log_dt, b_re, b_im, c_re, c_im, dvec):
    b, s, _ = proj.shape
    g, n = a_re.shape
    width = g * S5_GROUP
    gpb = LANES // S5_GROUP
    nblk = g // gpb
    nst = gpb * n
    p_re, p_im, cr, ci = _s5_prep(a_re, a_im, log_dt)
    bmat_re = _block_diag(jnp.transpose(b_re, (0, 2, 1)), nblk)
    bmat_im = _block_diag(jnp.transpose(b_im, (0, 2, 1)), nblk)
    cmat_re = _block_diag(jnp.transpose(c_re, (0, 2, 1)), nblk)
    cmat_im = _block_diag(jnp.transpose(c_im, (0, 2, 1)), nblk)
    ublk = col0 // LANES
    st_row = lambda rows: pl.BlockSpec((rows, nst), lambda i, j: (0, j))
    return pl.pallas_call(
        _s5_body,
        grid=(b, nblk),
        in_specs=[pl.BlockSpec((1, s, LANES), lambda i, j: (i, 0, ublk + j)),
                  pl.BlockSpec((1, LANES, nst), lambda i, j: (j, 0, 0)),
                  pl.BlockSpec((1, LANES, nst), lambda i, j: (j, 0, 0)),
                  pl.BlockSpec((1, nst, LANES), lambda i, j: (j, 0, 0)),
                  pl.BlockSpec((1, nst, LANES), lambda i, j: (j, 0, 0)),
                  st_row(SUBLANES), st_row(SUBLANES), st_row(1), st_row(1),
                  pl.BlockSpec((1, LANES), lambda i, j: (0, j))],
        out_specs=pl.BlockSpec((1, s, LANES), lambda i, j: (i, 0, j)),
        out_shape=jax.ShapeDtypeStruct((b, s, width), F32),
        scratch_shapes=[pltpu.VMEM((s, nst), F32), pltpu.VMEM((s, nst), F32)],
        compiler_params=_params(("parallel", "parallel")),
    )(proj, bmat_re, bmat_im, cmat_re, cmat_im, p_re.reshape(SUBLANES, g * n), p_im.reshape(SUBLANES, g * n),
      cr.reshape(1, g * n), ci.reshape(1, g * n), dvec.reshape(1, width))


def _glu_body(y_ref, w_ref, b_ref, o_ref):
    y = y_ref[...]
    o_ref[...] = (y * _sigmoid(_dot(y.astype(BF16), w_ref[...]) + b_ref[...])).astype(o_ref.dtype)


def _glu(y, w, bias, tm=512):
    t, d = y.shape
    tm = min(tm, t)
    return pl.pallas_call(
        _glu_body,
        grid=(t // tm,),
        in_specs=[pl.BlockSpec((tm, d), lambda i: (i, 0)), pl.BlockSpec((d, d), lambda i: (0, 0)),
                  pl.BlockSpec((1, d), lambda i: (0, 0))],
        out_specs=pl.BlockSpec((tm, d), lambda i: (i, 0)),
        out_shape=jax.ShapeDtypeStruct((t, d), BF16),
        compiler_params=_params(("parallel",)),
    )(y, w.astype(BF16), bias.reshape(1, d))


PEER_RANK_NONE = 127.0


def _top_rows(s, k, with_rank=False):
    rows = []
    rank = jnp.full(s.shape, PEER_RANK_NONE, F32) if with_rank else None
    for i in range(k):
        m = jnp.max(s, axis=0, keepdims=True)
        rows.append(m)
        hit = s == m
        if with_rank:
            rank = jnp.where(hit, float(i), rank)
        s = jnp.where(hit, NEG_BIG, s)
    return (rows, rank) if with_rank else rows


def _peer_route_body(q_ref, keys_ref, cnt1_ref, e1_ref, rank2_ref, e2_ref):
    half = keys_ref.shape[3]
    q = q_ref[...]
    s1 = _dot_nt(keys_ref[0, 0], q[:, :half], HIGHEST)
    s2 = _dot_nt(keys_ref[0, 1], q[:, half:], HIGHEST)
    nk = PEER_TOPK + 1
    v1 = _top_rows(s1, nk)
    v2_rows, rank2 = _top_rows(s2, nk, with_rank=True)
    pad = jnp.full((-nk % SUBLANES, q.shape[0]), NEG_BIG, F32)
    v2 = jnp.concatenate(v2_rows + [pad], axis=0)
    cand = [v1[0] + v2]
    for i in range(1, nk):
        need = nk // (i + 1)
        cand.append(v1[i] + v2[:-(-need // SUBLANES) * SUBLANES, :])
    top = _top_rows(jnp.concatenate(cand, axis=0), nk)
    zsum = jnp.ones_like(top[0])
    for r in top[1:PEER_TOPK]:
        zsum = zsum + jnp.exp(r - top[0])
    d1 = 0.5 * (top[PEER_TOPK - 1] + top[PEER_TOPK]) - s1
    cnt = jnp.zeros_like(s1)
    for r in v2_rows:
        cnt = cnt + jnp.where(r >= d1, 1.0, 0.0)
    cnt1_ref[0] = cnt
    e1_ref[0] = jnp.exp(s1 - v1[0])
    rank2_ref[0] = rank2.astype(BF16)
    e2_ref[0] = (jnp.exp(s2 - v2[0:1, :]) * (0.5 / zsum)).astype(BF16)


def _peer_route(q, keys, tt=512):
    t = q.shape[0]
    heads, _, nkeys, half = keys.shape
    tt = min(tt, t)
    table = pl.BlockSpec((1, nkeys, tt), lambda i, h: (h, 0, i))
    f32_shape = jax.ShapeDtypeStruct((heads, nkeys, t), F32)
    b16_shape = jax.ShapeDtypeStruct((heads, nkeys, t), BF16)
    return pl.pallas_call(
        _peer_route_body,
        grid=(t // tt, heads),
        in_specs=[pl.BlockSpec((tt, 2 * half), lambda i, h: (i, h)),
                  pl.BlockSpec((1, 2, nkeys, half), lambda i, h: (h, 0, 0, 0))],
        out_specs=[table] * 4,
        out_shape=[f32_shape, f32_shape, b16_shape, b16_shape],
        compiler_params=_params(("parallel", "parallel")),
    )(q, keys)


def _rows_bf16(row, n):
    one = jnp.broadcast_to(row, (BF16_ROWS, row.shape[1])).astype(BF16)
    return jnp.concatenate([one] * (n // BF16_ROWS), axis=0)


def _peer_dense_body(x_ref, u_ref, v_ref, cnt1_ref, e1_ref, rank2_ref, e2_ref, o_ref):
    heads, nkeys, _ = rank2_ref.shape
    te = u_ref.shape[0]
    j = pl.program_id(1)

    @pl.when(j == 0)
    def _():
        o_ref[...] = jnp.zeros_like(o_ref)

    pre = _dot_nt(u_ref[...], x_ref[...])
    act = pre * (1.0 + jnp.tanh(pre * (0.7978845608028654 + 0.035677408136300125 * (pre * pre))))
    parts = []
    for r in range(te // nkeys):
        i1 = j * (te // nkeys) + r
        w = None
        for h in range(heads):
            cnt = _rows_bf16(cnt1_ref[h, pl.ds(i1, 1), :], nkeys)
            e1 = _rows_bf16(e1_ref[h, pl.ds(i1, 1), :], nkeys)
            term = jnp.where(rank2_ref[h] < cnt, e2_ref[h] * e1, jnp.zeros((), BF16))
            w = term if w is None else w + term
        parts.append(w.astype(F32) * act[r * nkeys:(r + 1) * nkeys, :])
    p_t = jnp.concatenate(parts, axis=0) if len(parts) > 1 else parts[0]
    o_ref[...] += _dot(p_t.T.astype(BF16), v_ref[...])


def _peer_dense(x16, u16, v16, tables, tt=512, te=512):
    t, d = x16.shape
    e = u16.shape[0]
    heads, nkeys, _ = tables[0].shape
    tt = min(tt, t)
    te = min(te, e)
    assert te % nkeys == 0 and e == nkeys * nkeys
    once = pl.Buffered(1)
    table = pl.BlockSpec((heads, nkeys, tt), lambda i, j: (0, 0, i), pipeline_mode=once)
    return pl.pallas_call(
        _peer_dense_body,
        grid=(t // tt, e // te),
        in_specs=[pl.BlockSpec((tt, d), lambda i, j: (i, 0), pipeline_mode=once),
                  pl.BlockSpec((te, d), lambda i, j: (j, 0)),
                  pl.BlockSpec((te, d), lambda i, j: (j, 0)),
                  table, table, table, table],
        out_specs=pl.BlockSpec((tt, d), lambda i, j: (i, 0)),
        out_shape=jax.ShapeDtypeStruct((t, d), F32),
        compiler_params=_params(("parallel", "arbitrary")),
    )(x16, u16, v16, *tables)


def _peer_ffn(h16, wq16, keys, u16, v16):
    q = _matmul(h16, wq16)
    return _peer_dense(h16, u16, v16, _peer_route(q, keys))


def _even_mixer(x2d, b, s, j, w_in_all, gdn_conv_w, a_log, dt_bias, norm_w, rg_conv_w, rg_conv_b,
                rg_wa, rg_ba, rg_wx, rg_bx, rg_lambda, w_out_all):
    heads = a_log.shape[0]
    gw = heads * HEAD_DIM
    rw = rg_lambda.shape[0]
    ab_pad = LANES - 2 * heads
    w_in = w_in_all[j]
    w_qkvz = _stage_bf16(w_in_all, j, cols=4 * gw)
    w_ab = jnp.pad(w_in[:, 4 * gw:4 * gw + 2 * heads], ((0, 0), (0, ab_pad))).astype(BF16)
    w_rg = w_in[:, 4 * gw + 2 * heads:].astype(BF16)
    qkvz = _matmul(x2d, w_qkvz).reshape(b, s, 4 * gw)
    ab = _matmul(x2d, w_ab).reshape(b, s, LANES)
    rg = _matmul(x2d, w_rg).reshape(b, s, 2 * rw)
    gdn_out = _gdn(qkvz, ab, gdn_conv_w, a_log, dt_bias, norm_w, heads)
    rg_out = _rg_lru(rg, rg_conv_w, rg_conv_b, rg_wa, rg_wx, rg_ba, rg_bx, rg_lambda)
    return _matmul_pair(gdn_out.reshape(b * s, gw), rg_out.reshape(b * s, rw), _stage_bf16(w_out_all, j))


def _odd_mixer(x2d, b, s, j, w_in_all, a_re, a_im, log_dt, b_re, b_im, c_re, c_im, dvec, glu_w, glu_b,
               w_out_all):
    sw = dvec.shape[0]
    sbw = (w_in_all.shape[2] - sw) // 3
    heads = sbw // HEAD_DIM
    proj = _matmul(x2d, _stage_bf16(w_in_all, j)).reshape(b, s, 3 * sbw + sw)
    sb_out = _stick_breaking(proj, heads)
    yg = _s5(proj, 3 * sbw, a_re, a_im, log_dt, b_re, b_im, c_re, c_im, dvec)
    s5_out = _glu(yg.reshape(b * s, sw), glu_w, glu_b)
    return _matmul_pair(sb_out.reshape(b * s, sbw), s5_out, _stage_bf16(w_out_all, j))


def kernel(x, w_in_e, gdn_conv_w, gdn_A_log, gdn_dt_bias, gdn_norm_w, rg_conv_w, rg_conv_b, rg_wa, rg_ba,
           rg_wx, rg_bx, rg_lambda, w_out_e, w_in_o, s5_A_re, s5_A_im, s5_log_dt, s5_B_re, s5_B_im,
           s5_C_re, s5_C_im, s5_D, s5_glu_w, s5_glu_b, w_out_o, ln_mix_g, ln_mix_b, peer_wq, peer_keys,
           peer_u, peer_v, ln_ffn_g, ln_ffn_b):
    b, s, d = x.shape
    depth = ln_mix_g.shape[0]
    h = x.reshape(b * s, d)
    h16 = h.astype(BF16)
    for layer in range(depth):
        j = layer // 2
        if layer % 2 == 0:
            mix = _even_mixer(h16, b, s, j, w_in_e, gdn_conv_w[j], gdn_A_log[j], gdn_dt_bias[j], gdn_norm_w[j],
                              rg_conv_w[j], rg_conv_b[j], rg_wa[j], rg_ba[j], rg_wx[j], rg_bx[j],
                              rg_lambda[j], w_out_e)
        else:
            mix = _odd_mixer(h16, b, s, j, w_in_o, s5_A_re[j], s5_A_im[j], s5_log_dt[j], s5_B_re[j],
                             s5_B_im[j], s5_C_re[j], s5_C_im[j], s5_D[j], s5_glu_w[j], s5_glu_b[j], w_out_o)
        h, h16 = _residual_ln(h, mix, ln_mix_g[layer], ln_mix_b[layer])
        ffn = _peer_ffn(h16, _stage_bf16(peer_wq, layer), peer_keys[layer],
                        _stage_bf16(peer_u, layer), _stage_bf16(peer_v, layer))
        h, h16 = _residual_ln(h, ffn, ln_ffn_g[layer], ln_ffn_b[layer])
    return h.reshape(b, s, d)
```

```python
import functools
import math

import jax
import jax.numpy as jnp
from jax import lax
from jax.experimental import pallas as pl
from jax.experimental.pallas import tpu as pltpu

F32 = jnp.float32
BF16 = jnp.bfloat16
HIGHEST = lax.Precision.HIGHEST

LANES = 128
SUBLANES = 8
BF16_ROWS = 2 * SUBLANES
VMEM_BYTES_V7X = 64 * 1024 * 1024
VMEM_LIMIT = VMEM_BYTES_V7X - 8 * 1024 * 1024

HEAD_DIM = 128
CONV_W = 4
CONV_PAD = SUBLANES
GDN_CHUNK = 64
RG_BLOCK = 128
LRU_C = 8.0
S5_GROUP = 16
S5_STATE = 64
PEER_TOPK = 16
DEPTH = 2
DN_ALPHA = (2.0 * DEPTH) ** 0.25
LN_EPS = 1e-5
RMS_EPS = 1e-6
NEG_BIG = -3.0e38


def _params(sem, vmem=VMEM_LIMIT):
    return pltpu.CompilerParams(dimension_semantics=sem, vmem_limit_bytes=vmem)


def _softplus(x):
    return jnp.maximum(x, 0.0) + jnp.log1p(jnp.exp(-jnp.abs(x)))


def _sigmoid(x):
    return 1.0 / (1.0 + jnp.exp(-x))


def _gelu(x):
    return 0.5 * x * (1.0 + jnp.tanh(0.7978845608028654 * (x + 0.044715 * (x * x * x))))


def _dot(a, b, precision=None):
    return jnp.dot(a, b, preferred_element_type=F32, precision=precision)


def _split_bf16(x):
    hi = x.astype(BF16)
    return hi, (x - hi.astype(F32)).astype(BF16)


def _dot_split(a, b_hi, b_lo):
    a_hi, a_lo = _split_bf16(a)
    return _dot(a_hi, b_hi) + (_dot(a_lo, b_hi) + _dot(a_hi, b_lo))


def _dot_nt(a, b, precision=None):
    return lax.dot_general(a, b, (((1,), (1,)), ((), ())), preferred_element_type=F32, precision=precision)


def _dot_tn(a, b, precision=None):
    return lax.dot_general(a, b, (((0,), (0,)), ((), ())), preferred_element_type=F32, precision=precision)


def _mm_body(a_ref, b_ref, o_ref, acc_ref):
    k = pl.program_id(2)

    @pl.when(k == 0)
    def _():
        acc_ref[...] = jnp.zeros_like(acc_ref)

    acc_ref[...] += _dot(a_ref[...].astype(BF16), b_ref[...])

    @pl.when(k == pl.num_programs(2) - 1)
    def _():
        o_ref[...] = acc_ref[...].astype(o_ref.dtype)


def _mm_single_body(a_ref, b_ref, o_ref):
    o_ref[...] = _dot(a_ref[...].astype(BF16), b_ref[...]).astype(o_ref.dtype)


def _matmul(a, b, out_dtype=F32, tm=1024, tn=1024, tk=1024):
    m, kd = a.shape
    _, n = b.shape
    tm, tn = min(tm, m), min(tn, n)
    assert m % tm == 0 and n % tn == 0, (a.shape, b.shape)
    out_bytes = jnp.dtype(out_dtype).itemsize
    tile_bytes = 2 * (tm * kd * a.dtype.itemsize + kd * tn * b.dtype.itemsize + tm * tn * out_bytes)
    if tile_bytes <= (VMEM_LIMIT * 3) // 4:
        return pl.pallas_call(
            _mm_single_body,
            grid=(m // tm, n // tn),
            in_specs=[pl.BlockSpec((tm, kd), lambda i, j: (i, 0)),
                      pl.BlockSpec((kd, tn), lambda i, j: (0, j))],
            out_specs=pl.BlockSpec((tm, tn), lambda i, j: (i, j)),
            out_shape=jax.ShapeDtypeStruct((m, n), out_dtype),
            compiler_params=_params(("parallel", "parallel")),
        )(a, b)
    tk = min(tk, kd)
    assert kd % tk == 0, (a.shape, b.shape)
    return pl.pallas_call(
        _mm_body,
        grid=(m // tm, n // tn, kd // tk),
        in_specs=[pl.BlockSpec((tm, tk), lambda i, j, k: (i, k)),
                  pl.BlockSpec((tk, tn), lambda i, j, k: (k, j))],
        out_specs=pl.BlockSpec((tm, tn), lambda i, j, k: (i, j)),
        out_shape=jax.ShapeDtypeStruct((m, n), out_dtype),
        scratch_shapes=[pltpu.VMEM((tm, tn), F32)],
        compiler_params=_params(("parallel", "parallel", "arbitrary")),
    )(a, b)


def _mm_pair_body(a1_ref, a2_ref, b1_ref, b2_ref, o_ref):
    o_ref[...] = (_dot(a1_ref[...], b1_ref[...]) + _dot(a2_ref[...], b2_ref[...])).astype(o_ref.dtype)


def _matmul_pair(a1, a2, b, out_dtype=F32, tm=1024, tn=1024):
    m, k1 = a1.shape
    _, k2 = a2.shape
    _, n = b.shape
    tm, tn = min(tm, m), min(tn, n)
    assert m % tm == 0 and n % tn == 0 and k1 % k2 == 0 and b.shape[0] == k1 + k2
    return pl.pallas_call(
        _mm_pair_body,
        grid=(m // tm, n // tn),
        in_specs=[pl.BlockSpec((tm, k1), lambda i, j: (i, 0)),
                  pl.BlockSpec((tm, k2), lambda i, j: (i, 0)),
                  pl.BlockSpec((k1, tn), lambda i, j: (0, j)),
                  pl.BlockSpec((k2, tn), lambda i, j: (k1 // k2, j))],
        out_specs=pl.BlockSpec((tm, tn), lambda i, j: (i, j)),
        out_shape=jax.ShapeDtypeStruct((m, n), out_dtype),
        compiler_params=_params(("parallel", "parallel")),
    )(a1, a2, b, b)


def _cast_body(x_ref, o_ref):
    o_ref[...] = x_ref[0].astype(o_ref.dtype)


def _stage_bf16(x, layer, cols=None, tile_bytes=8 * 1024 * 1024):
    _, r, c = x.shape
    cols = c if cols is None else cols
    assert cols % LANES == 0
    tr = min(r, 1 << int(math.log2(tile_bytes // (cols * x.dtype.itemsize))))
    assert r % tr == 0 and tr % BF16_ROWS == 0
    return pl.pallas_call(
        _cast_body,
        grid=(r // tr,),
        in_specs=[pl.BlockSpec((1, tr, cols), lambda i: (layer, i, 0))],
        out_specs=pl.BlockSpec((tr, cols), lambda i: (i, 0)),
        out_shape=jax.ShapeDtypeStruct((r, cols), BF16),
        compiler_params=_params(("parallel",)),
    )(x)


def _ln_body(h_ref, d_ref, g_ref, b_ref, o_ref, ob_ref):
    y = DN_ALPHA * h_ref[...] + d_ref[...]
    mu = jnp.mean(y, axis=-1, keepdims=True)
    yc = y - mu
    var = jnp.mean(yc * yc, axis=-1, keepdims=True)
    out = yc * lax.rsqrt(var + LN_EPS) * g_ref[...] + b_ref[...]
    o_ref[...] = out
    ob_ref[...] = out.astype(BF16)


def _residual_ln(h, delta, g, b, tm=256):
    t, d = h.shape
    tm = min(tm, t)
    row = pl.BlockSpec((tm, d), lambda i: (i, 0))
    vec = pl.BlockSpec((1, d), lambda i: (0, 0))
    return pl.pallas_call(
        _ln_body,
        grid=(t // tm,),
        in_specs=[row, row, vec, vec],
        out_specs=[row, row],
        out_shape=[jax.ShapeDtypeStruct((t, d), F32), jax.ShapeDtypeStruct((t, d), BF16)],
        compiler_params=_params(("parallel",)),
    )(h, delta, g.reshape(1, d), b.reshape(1, d))


def _conv_rows(xpad, w, r, rows):
    acc = None
    for k in range(CONV_W):
        tap = xpad[pl.ds(r + CONV_PAD - (CONV_W - 1) + k, rows), :] * w[k:k + 1, :]
        acc = tap if acc is None else acc + tap
    return acc


def _row_tile(s):
    return min(s, 256)


def _bmm(a, b):
    return jnp.einsum("gmk,gkn->gmn", a.astype(BF16), b.astype(BF16), preferred_element_type=F32)


def _bmm_nt(a, b):
    return jnp.einsum("gmk,gnk->gmn", a.astype(BF16), b.astype(BF16), preferred_element_type=F32)


def _gdn_body(q_ref, k_ref, v_ref, z_ref, wq_ref, wk_ref, wv_ref, arow_ref, brow_ref,
              alog_ref, dtb_ref, nw_ref, o_ref, xq, xk, xv, state):
    ts = q_ref.shape[1]
    hb = q_ref.shape[2] // HEAD_DIM
    c = GDN_CHUNK
    nc = ts // c
    g = hb * nc

    @pl.when(pl.program_id(2) == 0)
    def _():
        for xp in (xq, xk, xv):
            xp[0:CONV_PAD, :] = jnp.zeros((CONV_PAD, hb * HEAD_DIM), F32)
        state[...] = jnp.zeros_like(state)

    def conv_act(x_ref, xp, w_ref, kind):
        xp[CONV_PAD:CONV_PAD + ts, :] = x_ref[0]
        w = w_ref[...]
        heads = []
        for h in range(hb):
            sl = slice(h * HEAD_DIM, (h + 1) * HEAD_DIM)
            y = None
            for k in range(CONV_W):
                tap = xp[pl.ds(CONV_PAD - (CONV_W - 1) + k, ts), sl] * w[k:k + 1, sl]
                y = tap if y is None else y + tap
            y = y * _sigmoid(y)
            if kind != "v":
                y = y * lax.rsqrt(jnp.sum(y * y, axis=-1, keepdims=True) + RMS_EPS)
            if kind == "q":
                y = y * (HEAD_DIM ** -0.5)
            heads.append(y.reshape(nc, c, HEAD_DIM))
        xp[0:CONV_PAD, :] = xp[ts:ts + CONV_PAD, :]
        return jnp.concatenate(heads, axis=0)

    qc = conv_act(q_ref, xq, wq_ref, "q")
    kc = conv_act(k_ref, xk, wk_ref, "k")
    vc = conv_act(v_ref, xv, wv_ref, "v")

    neg_a = -jnp.exp(alog_ref[...])[:, None]
    dtb = dtb_ref[...][:, None]
    grow = (neg_a * _softplus(arow_ref[0] + dtb)).reshape(g, 1, c)
    brow = _sigmoid(brow_ref[0]).reshape(g, 1, c)

    ri = lax.broadcasted_iota(jnp.int32, (c, c), 0)
    ci = lax.broadcasted_iota(jnp.int32, (c, c), 1)
    lower = ri >= ci
    diag = ri == ci
    gcol = jnp.sum(jnp.where(diag, grow, 0.0), axis=2, keepdims=True)
    beta = jnp.sum(jnp.where(diag, brow, 0.0), axis=2, keepdims=True)
    gc_col = jnp.sum(jnp.where(lower, grow, 0.0), axis=2, keepdims=True)
    gc_row = jnp.sum(jnp.where(ri <= ci, gcol, 0.0), axis=1, keepdims=True)
    gamma = jnp.where(lower, jnp.exp(jnp.where(lower, gc_col - gc_row, 0.0)), 0.0)
    kb = kc * beta
    p = jnp.where(ri > ci, -(_bmm_nt(kb, kc) * gamma), 0.0)
    r = p
    for _ in range(int(math.log2(c)) - 1):
        p = _bmm(p, p)
        r = r + p + _bmm(r, p)
    eg = jnp.exp(gc_col)
    vb = vc * beta
    kbe = kb * eg
    u = vb + _bmm(r, vb)
    w = kbe + _bmm(r, kbe)
    a_qk = _bmm_nt(qc, kc) * gamma
    g_last = gc_row[:, :, c - 1:c]
    k_tail = kc * jnp.exp(g_last - gc_col)
    q_dec = qc * eg
    decay = jnp.exp(g_last)

    per_head = lambda x: x.reshape((hb, nc) + x.shape[1:])
    u, w, a_qk, k_tail, q_dec, decay = [per_head(x) for x in (u, w, a_qk, k_tail, q_dec, decay)]
    nw = nw_ref[...]
    st = state[...]
    for i in range(nc):
        v_new = u[:, i] - _bmm(w[:, i], st)
        o = _bmm(q_dec[:, i], st) + _bmm(a_qk[:, i], v_new)
        kt = jnp.swapaxes(k_tail[:, i], 1, 2)
        st = st * decay[:, i] + _bmm(kt, v_new)
        o = o * lax.rsqrt(jnp.mean(o * o, axis=-1, keepdims=True) + RMS_EPS) * nw
        for h in range(hb):
            rows = slice(i * c, (i + 1) * c)
            cols = slice(h * HEAD_DIM, (h + 1) * HEAD_DIM)
            zz = z_ref[0, rows, cols]
            o_ref[0, rows, cols] = (o[h] * (zz * _sigmoid(zz))).astype(o_ref.dtype)
    state[...] = st


def _gdn(qkvz, ab, conv_w, a_log, dt_bias, norm_w, heads, heads_per_step=4, ts=512):
    b, s, _ = qkvz.shape
    c = GDN_CHUNK
    hb = min(heads_per_step, heads)
    ts = min(ts, s)
    ng = heads // hb
    rows = lambda x: jnp.transpose(x, (0, 2, 1)).reshape(b, heads, s // c, 1, c)
    arow = rows(ab[:, :, :heads])
    brow = rows(ab[:, :, heads:2 * heads])
    wide = hb * HEAD_DIM
    head = lambda off: pl.BlockSpec((1, ts, wide), lambda i, j, t: (i, t, off + j))
    cw = lambda off: pl.BlockSpec((CONV_W, wide), lambda i, j, t: (0, off + j))
    row = pl.BlockSpec((1, hb, ts // c, 1, c), lambda i, j, t: (i, j, t, 0, 0))
    scal = pl.BlockSpec((hb, 1, 1), lambda i, j, t: (j, 0, 0))
    return pl.pallas_call(
        _gdn_body,
        grid=(b, ng, s // ts),
        in_specs=[head(0), head(ng), head(2 * ng), head(3 * ng),
                  cw(0), cw(ng), cw(2 * ng), row, row,
                  scal, scal, pl.BlockSpec((1, HEAD_DIM), lambda i, j, t: (0, 0))],
        out_specs=pl.BlockSpec((1, ts, wide), lambda i, j, t: (i, t, j)),
        out_shape=jax.ShapeDtypeStruct((b, s, heads * HEAD_DIM), BF16),
        scratch_shapes=[pltpu.VMEM((ts + CONV_PAD, wide), F32)] * 3 + [pltpu.VMEM((hb, HEAD_DIM, HEAD_DIM), F32)],
        compiler_params=_params(("parallel", "parallel", "arbitrary")),
    )(qkvz, qkvz, qkvz, qkvz, conv_w, conv_w, conv_w, arow, brow,
      a_log.reshape(heads, 1, 1), dt_bias.reshape(heads, 1, 1), norm_w.reshape(1, HEAD_DIM))


def _shift_rows(x, d, fill):
    row = lax.broadcasted_iota(jnp.int32, x.shape, 0)
    return jnp.where(row >= d, pltpu.roll(x, d, 0), fill)


def _rg_body(x_ref, gate_ref, cw_ref, cb_ref, wa_ref, wx_ref, ba_ref, bx_ref, lam_ref, o_ref,
             xpad, a_s, h_s):
    s = x_ref.shape[1]
    cb = x_ref.shape[2]
    nb = cb // RG_BLOCK
    rt = _row_tile(s)
    xpad[0:CONV_PAD, :] = jnp.zeros((CONV_PAD, cb), F32)
    xpad[CONV_PAD:CONV_PAD + s, :] = x_ref[0]
    w = cw_ref[...]
    bias = cb_ref[...]
    lam_sp = _softplus(-lam_ref[...])
    for r in range(0, s, rt):
        xr = _conv_rows(xpad, w, r, rt) + bias
        for n in range(nb):
            sl = slice(n * RG_BLOCK, (n + 1) * RG_BLOCK)
            xb = xr[:, sl]
            xb16 = xb.astype(BF16)
            rgate = _sigmoid(_dot(xb16, wa_ref[n]) + ba_ref[:, sl])
            igate = _sigmoid(_dot(xb16, wx_ref[n]) + bx_ref[:, sl])
            log_a = -LRU_C * rgate * lam_sp[:, sl]
            a = jnp.exp(log_a)
            a_s[r:r + rt, sl] = a
            h_s[r:r + rt, sl] = jnp.sqrt(jnp.maximum(1.0 - a * a, 0.0)) * (igate * xb)

    def step(i, carry):
        r0 = pl.multiple_of(i * SUBLANES, SUBLANES)
        a = a_s[pl.ds(r0, SUBLANES), :]
        bv = h_s[pl.ds(r0, SUBLANES), :]
        for d in (1, 2, 4):
            bv = a * _shift_rows(bv, d, 0.0) + bv
            a = a * _shift_rows(a, d, 1.0)
        h = bv + a * carry
        h_s[pl.ds(r0, SUBLANES), :] = h
        return h[SUBLANES - 1:SUBLANES, :]

    lax.fori_loop(0, s // SUBLANES, step, jnp.zeros((1, cb), F32))
    for r in range(0, s, rt):
        o_ref[0, r:r + rt, :] = (h_s[r:r + rt, :] * _gelu(gate_ref[0, r:r + rt, :])).astype(o_ref.dtype)


def _rg_lru(rg, conv_w, conv_b, wa, wx, ba, bx, lam, blocks_per_step=4):
    b, s, w2 = rg.shape
    width = w2 // 2
    nblk = width // RG_BLOCK
    nb = min(blocks_per_step, nblk)
    cb = nb * RG_BLOCK
    steps = width // cb
    vec = pl.BlockSpec((1, cb), lambda i, j: (0, j))
    wblk = pl.BlockSpec((nb, RG_BLOCK, RG_BLOCK), lambda i, j: (j, 0, 0))
    return pl.pallas_call(
        _rg_body,
        grid=(b, steps),
        in_specs=[pl.BlockSpec((1, s, cb), lambda i, j: (i, 0, j)),
                  pl.BlockSpec((1, s, cb), lambda i, j: (i, 0, steps + j)),
                  pl.BlockSpec((CONV_W, cb), lambda i, j: (0, j)), vec, wblk, wblk, vec, vec, vec],
        out_specs=pl.BlockSpec((1, s, cb), lambda i, j: (i, 0, j)),
        out_shape=jax.ShapeDtypeStruct((b, s, width), BF16),
        scratch_shapes=[pltpu.VMEM((s + CONV_PAD, cb), F32), pltpu.VMEM((s, cb), F32), pltpu.VMEM((s, cb), F32)],
        compiler_params=_params(("parallel", "parallel")),
    )(rg, rg, conv_w, conv_b.reshape(1, width), wa.astype(BF16), wx.astype(BF16),
      ba.reshape(1, width), bx.reshape(1, width), lam.reshape(1, width))


SB_LOG_WEIGHT_FLOOR = -104.0


def _sb_body(q_ref, k_ref, v_ref, o_ref, *, bk):
    bq = q_ref.shape[1]
    hb = q_ref.shape[2] // HEAD_DIM
    qi = pl.program_id(2)
    t_pos = qi * bq + lax.broadcasted_iota(jnp.int32, (bq, bk), 0)
    s_off = lax.broadcasted_iota(jnp.int32, (bq, bk), 1)
    tri = (lax.broadcasted_iota(jnp.int32, (bk, bk), 0) > lax.broadcasted_iota(jnp.int32, (bk, bk), 1)).astype(BF16)
    nkb = (qi + 1) * (bq // bk)
    cols = [slice(h * HEAD_DIM, (h + 1) * HEAD_DIM) for h in range(hb)]
    q16 = [(q_ref[0, :, sl] * (HEAD_DIM ** -0.5)).astype(BF16) for sl in cols]

    def cond(carry):
        i, alive = carry[0], carry[1]
        return jnp.logical_and(i < nkb, alive)

    def body(carry):
        i, _, accs, runs = carry
        r0 = pl.multiple_of((nkb - 1 - i) * bk, bk)
        valid = (s_off + r0) < t_pos
        new_accs, new_runs = [], []
        top = None
        for h in range(hb):
            kblk = k_ref[0, pl.ds(r0, bk), cols[h]].astype(BF16)
            vblk = v_ref[0, pl.ds(r0, bk), cols[h]].astype(BF16)
            z = _dot_nt(q16[h], kblk)
            sp = _softplus(z)
            lm = jnp.where(valid, -sp, 0.0)
            hi = lm.astype(BF16)
            lo = (lm - hi.astype(F32)).astype(BF16)
            tail = _dot(hi, tri) + _dot(lo, tri) + runs[h]
            wts = jnp.where(valid, jnp.exp(z - sp + tail), 0.0)
            new_accs.append(accs[h] + _dot(wts.astype(BF16), vblk))
            run = runs[h] + jnp.sum(lm, axis=1, keepdims=True)
            new_runs.append(run)
            m = jnp.max(run)
            top = m if top is None else jnp.maximum(top, m)
        return i + 1, top > SB_LOG_WEIGHT_FLOOR, tuple(new_accs), tuple(new_runs)

    init = (jnp.int32(0), jnp.bool_(True),
            tuple(jnp.zeros((bq, HEAD_DIM), F32) for _ in range(hb)),
            tuple(jnp.zeros((bq, 1), F32) for _ in range(hb)))
    accs = lax.while_loop(cond, body, init)[2]
    for h in range(hb):
        o_ref[0, :, cols[h]] = accs[h].astype(o_ref.dtype)


def _stick_breaking(proj, heads, bq=256, bk=256, heads_per_step=2):
    b, s, _ = proj.shape
    bq = min(bq, s)
    bk = min(bk, bq)
    hb = min(heads_per_step, heads)
    ng = heads // hb
    wide = hb * HEAD_DIM
    return pl.pallas_call(
        functools.partial(_sb_body, bk=bk),
        grid=(b, ng, s // bq),
        in_specs=[pl.BlockSpec((1, bq, wide), lambda i, j, t: (i, t, j)),
                  pl.BlockSpec((1, s, wide), lambda i, j, t: (i, 0, ng + j)),
                  pl.BlockSpec((1, s, wide), lambda i, j, t: (i, 0, 2 * ng + j))],
        out_specs=pl.BlockSpec((1, bq, wide), lambda i, j, t: (i, t, j)),
        out_shape=jax.ShapeDtypeStruct((b, s, heads * HEAD_DIM), BF16),
        compiler_params=_params(("parallel", "parallel", "arbitrary")),
    )(proj, proj, proj)


def _s5_prep_body(are_ref, aim_ref, ldt_ref, abre_ref, abim_ref, cr_ref, ci_ref):
    dt = jnp.exp(ldt_ref[...])
    lr = jnp.minimum(are_ref[...], -1e-4)
    li = aim_ref[...]
    mag = jnp.exp(lr * dt)
    ab_re = mag * jnp.cos(li * dt)
    ab_im = mag * jnp.sin(li * dt)
    den = lr * lr + li * li
    nr = ab_re - 1.0
    cr_ref[...] = (nr * lr + ab_im * li) / den
    ci_ref[...] = (ab_im * lr - nr * li) / den
    p_re, p_im = ab_re, ab_im
    for k in range(SUBLANES):
        abre_ref[k] = p_re
        abim_ref[k] = p_im
        p_re, p_im = p_re * ab_re - p_im * ab_im, p_re * ab_im + p_im * ab_re


def _s5_prep(a_re, a_im, log_dt):
    g, n = a_re.shape
    full = pl.BlockSpec((g, n), lambda: (0, 0))
    pw = pl.BlockSpec((SUBLANES, g, n), lambda: (0, 0, 0))
    return pl.pallas_call(
        _s5_prep_body,
        in_specs=[full, full, pl.BlockSpec((g, 1), lambda: (0, 0))],
        out_specs=[pw, pw, full, full],
        out_shape=[jax.ShapeDtypeStruct((SUBLANES, g, n), F32)] * 2 + [jax.ShapeDtypeStruct((g, n), F32)] * 2,
    )(a_re, a_im, log_dt.reshape(g, 1))


def _s5_body(u_ref, bre_ref, bim_ref, cre_ref, cim_ref, pre_ref, pim_ref, cr_ref, ci_ref, d_ref, o_ref,
             hre_s, him_s):
    s = u_ref.shape[1]
    rt = _row_tile(s)
    cr = cr_ref[...]
    ci = ci_ref[...]
    b_re = _split_bf16(bre_ref[0])
    b_im = _split_bf16(bim_ref[0])
    for r in range(0, s, rt):
        u = u_ref[0, r:r + rt, :]
        xr = _dot_split(u, *b_re)
        xi = _dot_split(u, *b_im)
        hre_s[r:r + rt, :] = cr * xr - ci * xi
        him_s[r:r + rt, :] = cr * xi + ci * xr
    p_re = pre_ref[...]
    p_im = pim_ref[...]

    def step(i, carry):
        c_re, c_im = carry
        r0 = pl.multiple_of(i * SUBLANES, SUBLANES)
        x_re = hre_s[pl.ds(r0, SUBLANES), :]
        x_im = him_s[pl.ds(r0, SUBLANES), :]
        for d in (1, 2, 4):
            a_re = p_re[d - 1:d, :]
            a_im = p_im[d - 1:d, :]
            s_re = _shift_rows(x_re, d, 0.0)
            s_im = _shift_rows(x_im, d, 0.0)
            x_re, x_im = x_re + a_re * s_re - a_im * s_im, x_im + a_re * s_im + a_im * s_re
        h_re = x_re + p_re * c_re - p_im * c_im
        h_im = x_im + p_re * c_im + p_im * c_re
        hre_s[pl.ds(r0, SUBLANES), :] = h_re
        him_s[pl.ds(r0, SUBLANES), :] = h_im
        return h_re[SUBLANES - 1:SUBLANES, :], h_im[SUBLANES - 1:SUBLANES, :]

    zero = jnp.zeros((1, hre_s.shape[1]), F32)
    lax.fori_loop(0, s // SUBLANES, step, (zero, zero))
    dvec = d_ref[...]
    cmat_re = _split_bf16(cre_ref[0])
    cmat_im = _split_bf16(cim_ref[0])
    for r in range(0, s, rt):
        y = _dot_split(hre_s[r:r + rt, :], *cmat_re) - _dot_split(him_s[r:r + rt, :], *cmat_im)
        y = y + dvec * u_ref[0, r:r + rt, :]
        o_ref[0, r:r + rt, :] = _gelu(y)


def _block_diag(x, nblk):
    g = x.shape[0] // nblk
    r, c = x.shape[1:]
    eye = jnp.eye(g, dtype=x.dtype)
    return jnp.einsum("jgrc,gh->jgrhc", x.reshape(nblk, g, r, c), eye).reshape(nblk, g * r, g * c)


def _s5(proj, col0, a_re, a_im, log_dt, b_re, b_im, c_re, c_im, dvec):
    b, s, _ = proj.shape
    g, n = a_re.shape
    width = g * S5_GROUP
    gpb = LANES // S5_GROUP
    nblk = g // gpb
    nst = gpb * n
    p_re, p_im, cr, ci = _s5_prep(a_re, a_im, log_dt)
    bmat_re = _block_diag(jnp.transpose(b_re, (0, 2, 1)), nblk)
    bmat_im = _block_diag(jnp.transpose(b_im, (0, 2, 1)), nblk)
    cmat_re = _block_diag(jnp.transpose(c_re, (0, 2, 1)), nblk)
    cmat_im = _block_diag(jnp.transpose(c_im, (0, 2, 1)), nblk)
    ublk = col0 // LANES
    st_row = lambda rows: pl.BlockSpec((rows, nst), lambda i, j: (0, j))
    return pl.pallas_call(
        _s5_body,
        grid=(b, nblk),
        in_specs=[pl.BlockSpec((1, s, LANES), lambda i, j: (i, 0, ublk + j)),
                  pl.BlockSpec((1, LANES, nst), lambda i, j: (j, 0, 0)),
                  pl.BlockSpec((1, LANES, nst), lambda i, j: (j, 0, 0)),
                  pl.BlockSpec((1, nst, LANES), lambda i, j: (j, 0, 0)),
                  pl.BlockSpec((1, nst, LANES), lambda i, j: (j, 0, 0)),
                  st_row(SUBLANES), st_row(SUBLANES), st_row(1), st_row(1),
                  pl.BlockSpec((1, LANES), lambda i, j: (0, j))],
        out_specs=pl.BlockSpec((1, s, LANES), lambda i, j: (i, 0, j)),
        out_shape=jax.ShapeDtypeStruct((b, s, width), F32),
        scratch_shapes=[pltpu.VMEM((s, nst), F32), pltpu.VMEM((s, nst), F32)],
        compiler_params=_params(("parallel", "parallel")),
    )(proj, bmat_re, bmat_im, cmat_re, cmat_im, p_re.reshape(SUBLANES, g * n), p_im.reshape(SUBLANES, g * n),
      cr.reshape(1, g * n), ci.reshape(1, g * n), dvec.reshape(1, width))


def _glu_body(y_ref, w_ref, b_ref, o_ref):
    y = y_ref[...]
    o_ref[...] = (y * _sigmoid(_dot(y.astype(BF16), w_ref[...]) + b_ref[...])).astype(o_ref.dtype)


def _glu(y, w, bias, tm=512):
    t, d = y.shape
    tm = min(tm, t)
    return pl.pallas_call(
        _glu_body,
        grid=(t // tm,),
        in_specs=[pl.BlockSpec((tm, d), lambda i: (i, 0)), pl.BlockSpec((d, d), lambda i: (0, 0)),
                  pl.BlockSpec((1, d), lambda i: (0, 0))],
        out_specs=pl.BlockSpec((tm, d), lambda i: (i, 0)),
        out_shape=jax.ShapeDtypeStruct((t, d), BF16),
        compiler_params=_params(("parallel",)),
    )(y, w.astype(BF16), bias.reshape(1, d))


PEER_RANK_NONE = 127.0


def _top_rows(s, k, with_rank=False):
    rows = []
    rank = jnp.full(s.shape, PEER_RANK_NONE, F32) if with_rank else None
    for i in range(k):
        m = jnp.max(s, axis=0, keepdims=True)
        rows.append(m)
        hit = s == m
        if with_rank:
            rank = jnp.where(hit, float(i), rank)
        s = jnp.where(hit, NEG_BIG, s)
    return (rows, rank) if with_rank else rows


def _peer_route_lanes(s1, s2):
    nk = PEER_TOPK + 1
    v1 = _top_rows(s1, nk)
    v2_rows, rank2 = _top_rows(s2, nk, with_rank=True)
    pad = jnp.full((-nk % SUBLANES, s1.shape[1]), NEG_BIG, F32)
    v2 = jnp.concatenate(v2_rows + [pad], axis=0)
    cand = [v1[0] + v2]
    for i in range(1, nk):
        need = nk // (i + 1)
        cand.append(v1[i] + v2[:-(-need // SUBLANES) * SUBLANES, :])
    top = _top_rows(jnp.concatenate(cand, axis=0), nk)
    zsum = jnp.ones_like(top[0])
    for r in top[1:PEER_TOPK]:
        zsum = zsum + jnp.exp(r - top[0])
    d1 = 0.5 * (top[PEER_TOPK - 1] + top[PEER_TOPK]) - s1
    cnt = jnp.zeros_like(s1)
    for r in v2_rows:
        cnt = cnt + jnp.where(r >= d1, 1.0, 0.0)
    e1 = jnp.exp(s1 - v1[0])
    e2 = jnp.exp(s2 - v2[0:1, :]) * (0.5 / zsum)
    return cnt, e1, rank2.astype(BF16), e2.astype(BF16)


def _peer_route_body(q_ref, keys_ref, cnt1_ref, e1_ref, rank2_ref, e2_ref):
    half = keys_ref.shape[3]
    q = q_ref[...]
    s1 = _dot_nt(keys_ref[0, 0], q[:, :half], HIGHEST)
    s2 = _dot_nt(keys_ref[0, 1], q[:, half:], HIGHEST)
    for c in range(0, q.shape[0], LANES):
        tok = slice(c, c + LANES)
        cnt, e1, rank2, e2 = _peer_route_lanes(s1[:, tok], s2[:, tok])
        cnt1_ref[0, :, tok] = cnt
        e1_ref[0, :, tok] = e1
        rank2_ref[0, :, tok] = rank2
        e2_ref[0, :, tok] = e2


def _peer_route(q, keys, tt=512):
    t = q.shape[0]
    heads, _, nkeys, half = keys.shape
    tt = min(tt, t)
    table = pl.BlockSpec((1, nkeys, tt), lambda i, h: (h, 0, i))
    f32_shape = jax.ShapeDtypeStruct((heads, nkeys, t), F32)
    b16_shape = jax.ShapeDtypeStruct((heads, nkeys, t), BF16)
    return pl.pallas_call(
        _peer_route_body,
        grid=(t // tt, heads),
        in_specs=[pl.BlockSpec((tt, 2 * half), lambda i, h: (i, h)),
                  pl.BlockSpec((1, 2, nkeys, half), lambda i, h: (h, 0, 0, 0))],
        out_specs=[table] * 4,
        out_shape=[f32_shape, f32_shape, b16_shape, b16_shape],
        compiler_params=_params(("parallel", "parallel")),
    )(q, keys)


def _rows_bf16(row, n):
    one = jnp.broadcast_to(row, (BF16_ROWS, row.shape[1])).astype(BF16)
    return jnp.concatenate([one] * (n // BF16_ROWS), axis=0)


def _peer_dense_body(x_ref, u_ref, v_ref, cnt1_ref, e1_ref, rank2_ref, e2_ref, o_ref):
    heads, nkeys, _ = rank2_ref.shape
    te = u_ref.shape[0]
    j = pl.program_id(1)

    @pl.when(j == 0)
    def _():
        o_ref[...] = jnp.zeros_like(o_ref)

    pre = _dot_nt(u_ref[...], x_ref[...])
    act = pre * (1.0 + jnp.tanh(pre * (0.7978845608028654 + 0.035677408136300125 * (pre * pre))))
    parts = []
    for r in range(te // nkeys):
        i1 = j * (te // nkeys) + r
        w = None
        for h in range(heads):
            cnt = _rows_bf16(cnt1_ref[h, pl.ds(i1, 1), :], nkeys)
            e1 = _rows_bf16(e1_ref[h, pl.ds(i1, 1), :], nkeys)
            term = jnp.where(rank2_ref[h] < cnt, e2_ref[h] * e1, jnp.zeros((), BF16))
            w = term if w is None else w + term
        parts.append(w.astype(F32) * act[r * nkeys:(r + 1) * nkeys, :])
    p_t = jnp.concatenate(parts, axis=0) if len(parts) > 1 else parts[0]
    o_ref[...] += _dot(p_t.T.astype(BF16), v_ref[...])


def _peer_dense(x16, u16, v16, tables, tt=512, te=512):
    t, d = x16.shape
    e = u16.shape[0]
    heads, nkeys, _ = tables[0].shape
    tt = min(tt, t)
    te = min(te, e)
    assert te % nkeys == 0 and e == nkeys * nkeys
    table = pl.BlockSpec((heads, nkeys, tt), lambda i, j: (0, 0, i))
    return pl.pallas_call(
        _peer_dense_body,
        grid=(t // tt, e // te),
        in_specs=[pl.BlockSpec((tt, d), lambda i, j: (i, 0)),
                  pl.BlockSpec((te, d), lambda i, j: (j, 0)),
                  pl.BlockSpec((te, d), lambda i, j: (j, 0)),
                  table, table, table, table],
        out_specs=pl.BlockSpec((tt, d), lambda i, j: (i, 0)),
        out_shape=jax.ShapeDtypeStruct((t, d), F32),
        compiler_params=_params(("parallel", "arbitrary")),
    )(x16, u16, v16, *tables)


def _peer_ffn(h16, wq16, keys, u16, v16):
    q = _matmul(h16, wq16)
    return _peer_dense(h16, u16, v16, _peer_route(q, keys))


def _even_mixer(x2d, b, s, j, w_in_all, gdn_conv_w, a_log, dt_bias, norm_w, rg_conv_w, rg_conv_b,
                rg_wa, rg_ba, rg_wx, rg_bx, rg_lambda, w_out_all):
    heads = a_log.shape[0]
    gw = heads * HEAD_DIM
    rw = rg_lambda.shape[0]
    ab_pad = LANES - 2 * heads
    w_in = w_in_all[j]
    w_qkvz = _stage_bf16(w_in_all, j, cols=4 * gw)
    w_ab = jnp.pad(w_in[:, 4 * gw:4 * gw + 2 * heads], ((0, 0), (0, ab_pad))).astype(BF16)
    w_rg = w_in[:, 4 * gw + 2 * heads:].astype(BF16)
    qkvz = _matmul(x2d, w_qkvz).reshape(b, s, 4 * gw)
    ab = _matmul(x2d, w_ab).reshape(b, s, LANES)
    rg = _matmul(x2d, w_rg).reshape(b, s, 2 * rw)
    gdn_out = _gdn(qkvz, ab, gdn_conv_w, a_log, dt_bias, norm_w, heads)
    rg_out = _rg_lru(rg, rg_conv_w, rg_conv_b, rg_wa, rg_wx, rg_ba, rg_bx, rg_lambda)
    return _matmul_pair(gdn_out.reshape(b * s, gw), rg_out.reshape(b * s, rw), _stage_bf16(w_out_all, j))


def _odd_mixer(x2d, b, s, j, w_in_all, a_re, a_im, log_dt, b_re, b_im, c_re, c_im, dvec, glu_w, glu_b,
               w_out_all):
    sw = dvec.shape[0]
    sbw = (w_in_all.shape[2] - sw) // 3
    heads = sbw // HEAD_DIM
    proj = _matmul(x2d, _stage_bf16(w_in_all, j)).reshape(b, s, 3 * sbw + sw)
    sb_out = _stick_breaking(proj, heads)
    yg = _s5(proj, 3 * sbw, a_re, a_im, log_dt, b_re, b_im, c_re, c_im, dvec)
    s5_out = _glu(yg.reshape(b * s, sw), glu_w, glu_b)
    return _matmul_pair(sb_out.reshape(b * s, sbw), s5_out, _stage_bf16(w_out_all, j))


def kernel(x, w_in_e, gdn_conv_w, gdn_A_log, gdn_dt_bias, gdn_norm_w, rg_conv_w, rg_conv_b, rg_wa, rg_ba,
           rg_wx, rg_bx, rg_lambda, w_out_e, w_in_o, s5_A_re, s5_A_im, s5_log_dt, s5_B_re, s5_B_im,
           s5_C_re, s5_C_im, s5_D, s5_glu_w, s5_glu_b, w_out_o, ln_mix_g, ln_mix_b, peer_wq, peer_keys,
           peer_u, peer_v, ln_ffn_g, ln_ffn_b):
    b, s, d = x.shape
    depth = ln_mix_g.shape[0]
    h = x.reshape(b * s, d)
    h16 = h.astype(BF16)
    for layer in range(depth):
        j = layer // 2
        if layer % 2 == 0:
            mix = _even_mixer(h16, b, s, j, w_in_e, gdn_conv_w[j], gdn_A_log[j], gdn_dt_bias[j], gdn_norm_w[j],
                              rg_conv_w[j], rg_conv_b[j], rg_wa[j], rg_ba[j], rg_wx[j], rg_bx[j],
                              rg_lambda[j], w_out_e)
        else:
            mix = _odd_mixer(h16, b, s, j, w_in_o, s5_A_re[j], s5_A_im[j], s5_log_dt[j], s5_B_re[j],
                             s5_B_im[j], s5_C_re[j], s5_C_im[j], s5_D[j], s5_glu_w[j], s5_glu_b[j], w_out_o)
        h, h16 = _residual_ln(h, mix, ln_mix_g[layer], ln_mix_b[layer])
        ffn = _peer_ffn(h16, _stage_bf16(peer_wq, layer), peer_keys[layer],
                        _stage_bf16(peer_u, layer), _stage_bf16(peer_v, layer))
        h, h16 = _residual_ln(h, ffn, ln_ffn_g[layer], ln_ffn_b[layer])
    return h.reshape(b, s, d)
```

```python
import functools
import math

import jax
import jax.numpy as jnp
from jax import lax
from jax.experimental import pallas as pl
from jax.experimental.pallas import tpu as pltpu

F32 = jnp.float32
BF16 = jnp.bfloat16
HIGHEST = lax.Precision.HIGHEST

LANES = 128
SUBLANES = 8
BF16_ROWS = 2 * SUBLANES
VMEM_BYTES_V7X = 64 * 1024 * 1024
VMEM_LIMIT = VMEM_BYTES_V7X - 8 * 1024 * 1024

HEAD_DIM = 128
CONV_W = 4
CONV_PAD = SUBLANES
GDN_CHUNK = 64
RG_BLOCK = 128
LRU_C = 8.0
S5_GROUP = 16
S5_STATE = 64
PEER_TOPK = 16
DEPTH = 2
DN_ALPHA = (2.0 * DEPTH) ** 0.25
LN_EPS = 1e-5
RMS_EPS = 1e-6
NEG_BIG = -3.0e38


def _params(sem, vmem=VMEM_LIMIT):
    return pltpu.CompilerParams(dimension_semantics=sem, vmem_limit_bytes=vmem)


def _softplus(x):
    return jnp.maximum(x, 0.0) + jnp.log1p(jnp.exp(-jnp.abs(x)))


def _sigmoid(x):
    return 1.0 / (1.0 + jnp.exp(-x))


def _gelu(x):
    return 0.5 * x * (1.0 + jnp.tanh(0.7978845608028654 * (x + 0.044715 * (x * x * x))))


def _dot(a, b, precision=None):
    return jnp.dot(a, b, preferred_element_type=F32, precision=precision)


def _split_bf16(x):
    hi = x.astype(BF16)
    return hi, (x - hi.astype(F32)).astype(BF16)


def _dot_split(a, b_hi, b_lo):
    a_hi, a_lo = _split_bf16(a)
    return _dot(a_hi, b_hi) + (_dot(a_lo, b_hi) + _dot(a_hi, b_lo))


def _dot_nt(a, b, precision=None):
    return lax.dot_general(a, b, (((1,), (1,)), ((), ())), preferred_element_type=F32, precision=precision)


def _dot_tn(a, b, precision=None):
    return lax.dot_general(a, b, (((0,), (0,)), ((), ())), preferred_element_type=F32, precision=precision)


def _mm_body(a_ref, b_ref, o_ref, acc_ref):
    k = pl.program_id(2)

    @pl.when(k == 0)
    def _():
        acc_ref[...] = jnp.zeros_like(acc_ref)

    acc_ref[...] += _dot(a_ref[...].astype(BF16), b_ref[...])

    @pl.when(k == pl.num_programs(2) - 1)
    def _():
        o_ref[...] = acc_ref[...].astype(o_ref.dtype)


def _mm_single_body(a_ref, b_ref, o_ref):
    o_ref[...] = _dot(a_ref[...].astype(BF16), b_ref[...]).astype(o_ref.dtype)


def _matmul(a, b, out_dtype=F32, tm=1024, tn=1024, tk=1024):
    m, kd = a.shape
    _, n = b.shape
    tm, tn = min(tm, m), min(tn, n)
    assert m % tm == 0 and n % tn == 0, (a.shape, b.shape)
    out_bytes = jnp.dtype(out_dtype).itemsize
    tile_bytes = 2 * (tm * kd * a.dtype.itemsize + kd * tn * b.dtype.itemsize + tm * tn * out_bytes)
    if tile_bytes <= (VMEM_LIMIT * 3) // 4:
        return pl.pallas_call(
            _mm_single_body,
            grid=(m // tm, n // tn),
            in_specs=[pl.BlockSpec((tm, kd), lambda i, j: (i, 0)),
                      pl.BlockSpec((kd, tn), lambda i, j: (0, j))],
            out_specs=pl.BlockSpec((tm, tn), lambda i, j: (i, j)),
            out_shape=jax.ShapeDtypeStruct((m, n), out_dtype),
            compiler_params=_params(("parallel", "parallel")),
        )(a, b)
    tk = min(tk, kd)
    assert kd % tk == 0, (a.shape, b.shape)
    return pl.pallas_call(
        _mm_body,
        grid=(m // tm, n // tn, kd // tk),
        in_specs=[pl.BlockSpec((tm, tk), lambda i, j, k: (i, k)),
                  pl.BlockSpec((tk, tn), lambda i, j, k: (k, j))],
        out_specs=pl.BlockSpec((tm, tn), lambda i, j, k: (i, j)),
        out_shape=jax.ShapeDtypeStruct((m, n), out_dtype),
        scratch_shapes=[pltpu.VMEM((tm, tn), F32)],
        compiler_params=_params(("parallel", "parallel", "arbitrary")),
    )(a, b)


def _mm_nt_body(a_ref, bt_ref, o_ref):
    o_ref[...] = _dot_nt(a_ref[...], bt_ref[...]).astype(o_ref.dtype)


def _matmul_nt(a, bt, n=None, out_dtype=F32, tm=1024, tn=1024):
    m, kd = a.shape
    n = bt.shape[0] if n is None else n
    tm, tn = min(tm, m), min(tn, n)
    assert m % tm == 0 and n % tn == 0 and bt.shape[1] == kd
    return pl.pallas_call(
        _mm_nt_body,
        grid=(m // tm, n // tn),
        in_specs=[pl.BlockSpec((tm, kd), lambda i, j: (i, 0)),
                  pl.BlockSpec((tn, kd), lambda i, j: (j, 0))],
        out_specs=pl.BlockSpec((tm, tn), lambda i, j: (i, j)),
        out_shape=jax.ShapeDtypeStruct((m, n), out_dtype),
        compiler_params=_params(("parallel", "parallel")),
    )(a, bt)


def _mm_pair_body(a1_ref, a2_ref, b1_ref, b2_ref, o_ref):
    o_ref[...] = (_dot(a1_ref[...], b1_ref[...]) + _dot(a2_ref[...], b2_ref[...])).astype(o_ref.dtype)


def _matmul_pair(a1, a2, b, out_dtype=F32, tm=1024, tn=1024):
    m, k1 = a1.shape
    _, k2 = a2.shape
    _, n = b.shape
    tm, tn = min(tm, m), min(tn, n)
    assert m % tm == 0 and n % tn == 0 and k1 % k2 == 0 and b.shape[0] == k1 + k2
    return pl.pallas_call(
        _mm_pair_body,
        grid=(m // tm, n // tn),
        in_specs=[pl.BlockSpec((tm, k1), lambda i, j: (i, 0)),
                  pl.BlockSpec((tm, k2), lambda i, j: (i, 0)),
                  pl.BlockSpec((k1, tn), lambda i, j: (0, j)),
                  pl.BlockSpec((k2, tn), lambda i, j: (k1 // k2, j))],
        out_specs=pl.BlockSpec((tm, tn), lambda i, j: (i, j)),
        out_shape=jax.ShapeDtypeStruct((m, n), out_dtype),
        compiler_params=_params(("parallel", "parallel")),
    )(a1, a2, b, b)


def _cast_body(x_ref, o_ref):
    o_ref[...] = x_ref[0].astype(o_ref.dtype)


def _stage_bf16(x, layer, tile_bytes=8 * 1024 * 1024):
    _, r, cols = x.shape
    assert cols % LANES == 0
    limit = tile_bytes // (cols * x.dtype.itemsize)
    tr = max([t for t in range(BF16_ROWS, min(r, limit) + 1, BF16_ROWS) if r % t == 0], default=r)
    return pl.pallas_call(
        _cast_body,
        grid=(r // tr,),
        in_specs=[pl.BlockSpec((1, tr, cols), lambda i: (layer, i, 0))],
        out_specs=pl.BlockSpec((tr, cols), lambda i: (i, 0)),
        out_shape=jax.ShapeDtypeStruct((r, cols), BF16),
        compiler_params=_params(("parallel",)),
    )(x)


def _ln_body(h_ref, d_ref, g_ref, b_ref, o_ref, ob_ref):
    y = DN_ALPHA * h_ref[...] + d_ref[...].astype(F32)
    mu = jnp.mean(y, axis=-1, keepdims=True)
    yc = y - mu
    var = jnp.mean(yc * yc, axis=-1, keepdims=True)
    out = yc * lax.rsqrt(var + LN_EPS) * g_ref[...] + b_ref[...]
    o_ref[...] = out
    ob_ref[...] = out.astype(BF16)


def _residual_ln(h, delta, g, b, tm=256):
    t, d = h.shape
    tm = min(tm, t)
    row = pl.BlockSpec((tm, d), lambda i: (i, 0))
    vec = pl.BlockSpec((1, d), lambda i: (0, 0))
    return pl.pallas_call(
        _ln_body,
        grid=(t // tm,),
        in_specs=[row, row, vec, vec],
        out_specs=[row, row],
        out_shape=[jax.ShapeDtypeStruct((t, d), F32), jax.ShapeDtypeStruct((t, d), BF16)],
        compiler_params=_params(("parallel",)),
    )(h, delta, g.reshape(1, d), b.reshape(1, d))


def _conv_rows(xpad, w, r, rows):
    acc = None
    for k in range(CONV_W):
        tap = xpad[pl.ds(r + CONV_PAD - (CONV_W - 1) + k, rows), :] * w[k:k + 1, :]
        acc = tap if acc is None else acc + tap
    return acc


def _row_tile(s):
    return min(s, 256)


def _bmm(a, b):
    return jnp.einsum("gmk,gkn->gmn", a.astype(BF16), b.astype(BF16), preferred_element_type=F32)


def _bmm_nt(a, b):
    return jnp.einsum("gmk,gnk->gmn", a.astype(BF16), b.astype(BF16), preferred_element_type=F32)


def _gdn_body(q_ref, k_ref, v_ref, z_ref, wq_ref, wk_ref, wv_ref, arow_ref, brow_ref,
              alog_ref, dtb_ref, nw_ref, o_ref, xq, xk, xv, state):
    ts = q_ref.shape[1]
    hb = q_ref.shape[2] // HEAD_DIM
    c = GDN_CHUNK
    nc = ts // c
    g = hb * nc

    @pl.when(pl.program_id(2) == 0)
    def _():
        for xp in (xq, xk, xv):
            xp[0:CONV_PAD, :] = jnp.zeros((CONV_PAD, hb * HEAD_DIM), F32)
        state[...] = jnp.zeros_like(state)

    def conv_act(x_ref, xp, w_ref, kind):
        xp[CONV_PAD:CONV_PAD + ts, :] = x_ref[0]
        w = w_ref[...]
        heads = []
        for h in range(hb):
            sl = slice(h * HEAD_DIM, (h + 1) * HEAD_DIM)
            y = None
            for k in range(CONV_W):
                tap = xp[pl.ds(CONV_PAD - (CONV_W - 1) + k, ts), sl] * w[k:k + 1, sl]
                y = tap if y is None else y + tap
            y = y * _sigmoid(y)
            if kind != "v":
                y = y * lax.rsqrt(jnp.sum(y * y, axis=-1, keepdims=True) + RMS_EPS)
            if kind == "q":
                y = y * (HEAD_DIM ** -0.5)
            heads.append(y.reshape(nc, c, HEAD_DIM))
        xp[0:CONV_PAD, :] = xp[ts:ts + CONV_PAD, :]
        return jnp.concatenate(heads, axis=0)

    qc = conv_act(q_ref, xq, wq_ref, "q")
    kc = conv_act(k_ref, xk, wk_ref, "k")
    vc = conv_act(v_ref, xv, wv_ref, "v")

    neg_a = -jnp.exp(alog_ref[...])[:, None]
    dtb = dtb_ref[...][:, None]
    grow = (neg_a * _softplus(arow_ref[0] + dtb)).reshape(g, 1, c)
    brow = _sigmoid(brow_ref[0]).reshape(g, 1, c)

    ri = lax.broadcasted_iota(jnp.int32, (c, c), 0)
    ci = lax.broadcasted_iota(jnp.int32, (c, c), 1)
    lower = ri >= ci
    diag = ri == ci
    gcol = jnp.sum(jnp.where(diag, grow, 0.0), axis=2, keepdims=True)
    beta = jnp.sum(jnp.where(diag, brow, 0.0), axis=2, keepdims=True)
    gc_col = jnp.sum(jnp.where(lower, grow, 0.0), axis=2, keepdims=True)
    gc_row = jnp.sum(jnp.where(ri <= ci, gcol, 0.0), axis=1, keepdims=True)
    gamma = jnp.where(lower, jnp.exp(jnp.where(lower, gc_col - gc_row, 0.0)), 0.0)
    kb = kc * beta
    p = jnp.where(ri > ci, -(_bmm_nt(kb, kc) * gamma), 0.0)
    r = p
    for _ in range(int(math.log2(c)) - 1):
        p = _bmm(p, p)
        r = r + p + _bmm(r, p)
    eg = jnp.exp(gc_col)
    vb = vc * beta
    kbe = kb * eg
    u = vb + _bmm(r, vb)
    w = kbe + _bmm(r, kbe)
    a_qk = _bmm_nt(qc, kc) * gamma
    g_last = gc_row[:, :, c - 1:c]
    k_tail = kc * jnp.exp(g_last - gc_col)
    q_dec = qc * eg
    decay = jnp.exp(g_last)

    per_head = lambda x: x.reshape((hb, nc) + x.shape[1:])
    u, w, a_qk, k_tail, q_dec, decay = [per_head(x) for x in (u, w, a_qk, k_tail, q_dec, decay)]
    nw = nw_ref[...]
    st = state[...]
    for i in range(nc):
        v_new = u[:, i] - _bmm(w[:, i], st)
        o = _bmm(q_dec[:, i], st) + _bmm(a_qk[:, i], v_new)
        kt = jnp.swapaxes(k_tail[:, i], 1, 2)
        st = st * decay[:, i] + _bmm(kt, v_new)
        o = o * lax.rsqrt(jnp.mean(o * o, axis=-1, keepdims=True) + RMS_EPS) * nw
        for h in range(hb):
            rows = slice(i * c, (i + 1) * c)
            cols = slice(h * HEAD_DIM, (h + 1) * HEAD_DIM)
            zz = z_ref[0, rows, cols]
            o_ref[0, rows, cols] = (o[h] * (zz * _sigmoid(zz))).astype(o_ref.dtype)
    state[...] = st


def _gdn(qkvz, ab, conv_w, a_log, dt_bias, norm_w, heads, heads_per_step=8, ts=256):
    b, s, _ = qkvz.shape
    c = GDN_CHUNK
    hb = min(heads_per_step, heads)
    ts = min(ts, s)
    ng = heads // hb
    rows = lambda x: jnp.transpose(x, (0, 2, 1)).reshape(b, heads, s // c, 1, c)
    arow = rows(ab[:, :, :heads])
    brow = rows(ab[:, :, heads:2 * heads])
    wide = hb * HEAD_DIM
    head = lambda off: pl.BlockSpec((1, ts, wide), lambda i, j, t: (i, t, off + j))
    cw = lambda off: pl.BlockSpec((CONV_W, wide), lambda i, j, t: (0, off + j))
    row = pl.BlockSpec((1, hb, ts // c, 1, c), lambda i, j, t: (i, j, t, 0, 0))
    scal = pl.BlockSpec((hb, 1, 1), lambda i, j, t: (j, 0, 0))
    return pl.pallas_call(
        _gdn_body,
        grid=(b, ng, s // ts),
        in_specs=[head(0), head(ng), head(2 * ng), head(3 * ng),
                  cw(0), cw(ng), cw(2 * ng), row, row,
                  scal, scal, pl.BlockSpec((1, HEAD_DIM), lambda i, j, t: (0, 0))],
        out_specs=pl.BlockSpec((1, ts, wide), lambda i, j, t: (i, t, j)),
        out_shape=jax.ShapeDtypeStruct((b, s, heads * HEAD_DIM), BF16),
        scratch_shapes=[pltpu.VMEM((ts + CONV_PAD, wide), F32)] * 3 + [pltpu.VMEM((hb, HEAD_DIM, HEAD_DIM), F32)],
        compiler_params=_params(("parallel", "parallel", "arbitrary")),
    )(qkvz, qkvz, qkvz, qkvz, conv_w, conv_w, conv_w, arow, brow,
      a_log.reshape(heads, 1, 1), dt_bias.reshape(heads, 1, 1), norm_w.reshape(1, HEAD_DIM))


SCAN_ROWS = SUBLANES
SCAN_SHIFTS = tuple(1 << i for i in range(int(math.log2(SCAN_ROWS))))


def _shift_rows(x, d, fill):
    row = lax.broadcasted_iota(jnp.int32, x.shape, 0)
    return jnp.where(row >= d, pltpu.roll(x, d, 0), fill)


def _rg_body(x_ref, gate_ref, cw_ref, cb_ref, wa_ref, wx_ref, ba_ref, bx_ref, lam_ref, o_ref,
             xpad, a_s, h_s):
    s = x_ref.shape[1]
    cb = x_ref.shape[2]
    nb = cb // RG_BLOCK
    rt = _row_tile(s)
    xpad[0:CONV_PAD, :] = jnp.zeros((CONV_PAD, cb), F32)
    xpad[CONV_PAD:CONV_PAD + s, :] = x_ref[0]
    w = cw_ref[...]
    bias = cb_ref[...]
    lam_sp = _softplus(-lam_ref[...])
    for r in range(0, s, rt):
        xr = _conv_rows(xpad, w, r, rt) + bias
        for n in range(nb):
            sl = slice(n * RG_BLOCK, (n + 1) * RG_BLOCK)
            xb = xr[:, sl]
            xb16 = xb.astype(BF16)
            rgate = _sigmoid(_dot(xb16, wa_ref[n]) + ba_ref[:, sl])
            igate = _sigmoid(_dot(xb16, wx_ref[n]) + bx_ref[:, sl])
            log_a = -LRU_C * rgate * lam_sp[:, sl]
            a = jnp.exp(log_a)
            a_s[r:r + rt, sl] = a
            h_s[r:r + rt, sl] = jnp.sqrt(jnp.maximum(1.0 - a * a, 0.0)) * (igate * xb)

    def step(i, carry):
        r0 = pl.multiple_of(i * SCAN_ROWS, SCAN_ROWS)
        a = a_s[pl.ds(r0, SCAN_ROWS), :]
        bv = h_s[pl.ds(r0, SCAN_ROWS), :]
        for d in SCAN_SHIFTS:
            bv = a * _shift_rows(bv, d, 0.0) + bv
            a = a * _shift_rows(a, d, 1.0)
        h = bv + a * carry
        h_s[pl.ds(r0, SCAN_ROWS), :] = h
        return h[SCAN_ROWS - 1:SCAN_ROWS, :]

    lax.fori_loop(0, s // SCAN_ROWS, step, jnp.zeros((1, cb), F32))
    for r in range(0, s, rt):
        o_ref[0, r:r + rt, :] = (h_s[r:r + rt, :] * _gelu(gate_ref[0, r:r + rt, :])).astype(o_ref.dtype)


def _rg_lru(rg, conv_w, conv_b, wa, wx, ba, bx, lam, blocks_per_step=4):
    b, s, w2 = rg.shape
    width = w2 // 2
    nblk = width // RG_BLOCK
    nb = min(blocks_per_step, nblk)
    cb = nb * RG_BLOCK
    steps = width // cb
    vec = pl.BlockSpec((1, cb), lambda i, j: (0, j))
    wblk = pl.BlockSpec((nb, RG_BLOCK, RG_BLOCK), lambda i, j: (j, 0, 0))
    return pl.pallas_call(
        _rg_body,
        grid=(b, steps),
        in_specs=[pl.BlockSpec((1, s, cb), lambda i, j: (i, 0, j)),
                  pl.BlockSpec((1, s, cb), lambda i, j: (i, 0, steps + j)),
                  pl.BlockSpec((CONV_W, cb), lambda i, j: (0, j)), vec, wblk, wblk, vec, vec, vec],
        out_specs=pl.BlockSpec((1, s, cb), lambda i, j: (i, 0, j)),
        out_shape=jax.ShapeDtypeStruct((b, s, width), BF16),
        scratch_shapes=[pltpu.VMEM((s + CONV_PAD, cb), F32), pltpu.VMEM((s, cb), F32), pltpu.VMEM((s, cb), F32)],
        compiler_params=_params(("parallel", "parallel")),
    )(rg, rg, conv_w, conv_b.reshape(1, width), wa.astype(BF16), wx.astype(BF16),
      ba.reshape(1, width), bx.reshape(1, width), lam.reshape(1, width))


SB_LOG_WEIGHT_FLOOR = -104.0


def _sb_body(q_ref, k_ref, v_ref, o_ref, *, bk):
    bq = q_ref.shape[1]
    hb = q_ref.shape[2] // HEAD_DIM
    qi = pl.program_id(2)
    t_pos = qi * bq + lax.broadcasted_iota(jnp.int32, (bq, bk), 0)
    s_off = lax.broadcasted_iota(jnp.int32, (bq, bk), 1)
    tri = (lax.broadcasted_iota(jnp.int32, (bk, bk), 0) > lax.broadcasted_iota(jnp.int32, (bk, bk), 1)).astype(BF16)
    nkb = (qi + 1) * (bq // bk)
    cols = [slice(h * HEAD_DIM, (h + 1) * HEAD_DIM) for h in range(hb)]
    q16 = [(q_ref[0, :, sl] * (HEAD_DIM ** -0.5)).astype(BF16) for sl in cols]

    def cond(carry):
        i, alive = carry[0], carry[1]
        return jnp.logical_and(i < nkb, alive)

    def body(carry):
        i, _, accs, runs = carry
        r0 = pl.multiple_of((nkb - 1 - i) * bk, bk)
        valid = (s_off + r0) < t_pos
        new_accs, new_runs = [], []
        top = None
        for h in range(hb):
            kblk = k_ref[0, pl.ds(r0, bk), cols[h]].astype(BF16)
            vblk = v_ref[0, pl.ds(r0, bk), cols[h]].astype(BF16)
            z = _dot_nt(q16[h], kblk)
            sp = _softplus(z)
            lm = jnp.where(valid, -sp, 0.0)
            hi = lm.astype(BF16)
            lo = (lm - hi.astype(F32)).astype(BF16)
            tail = _dot(hi, tri) + _dot(lo, tri) + runs[h]
            wts = jnp.where(valid, jnp.exp(z - sp + tail), 0.0)
            new_accs.append(accs[h] + _dot(wts.astype(BF16), vblk))
            run = runs[h] + jnp.sum(lm, axis=1, keepdims=True)
            new_runs.append(run)
            m = jnp.max(run)
            top = m if top is None else jnp.maximum(top, m)
        return i + 1, top > SB_LOG_WEIGHT_FLOOR, tuple(new_accs), tuple(new_runs)

    init = (jnp.int32(0), jnp.bool_(True),
            tuple(jnp.zeros((bq, HEAD_DIM), F32) for _ in range(hb)),
            tuple(jnp.zeros((bq, 1), F32) for _ in range(hb)))
    accs = lax.while_loop(cond, body, init)[2]
    for h in range(hb):
        o_ref[0, :, cols[h]] = accs[h].astype(o_ref.dtype)


def _stick_breaking(proj, heads, bq=256, bk=256, heads_per_step=2):
    b, s, _ = proj.shape
    bq = min(bq, s)
    bk = min(bk, bq)
    hb = min(heads_per_step, heads)
    ng = heads // hb
    wide = hb * HEAD_DIM
    return pl.pallas_call(
        functools.partial(_sb_body, bk=bk),
        grid=(b, ng, s // bq),
        in_specs=[pl.BlockSpec((1, bq, wide), lambda i, j, t: (i, t, j)),
                  pl.BlockSpec((1, s, wide), lambda i, j, t: (i, 0, ng + j)),
                  pl.BlockSpec((1, s, wide), lambda i, j, t: (i, 0, 2 * ng + j))],
        out_specs=pl.BlockSpec((1, bq, wide), lambda i, j, t: (i, t, j)),
        out_shape=jax.ShapeDtypeStruct((b, s, heads * HEAD_DIM), BF16),
        compiler_params=_params(("parallel", "parallel", "arbitrary")),
    )(proj, proj, proj)


def _s5_prep_body(are_ref, aim_ref, ldt_ref, abre_ref, abim_ref, cr_ref, ci_ref):
    dt = jnp.exp(ldt_ref[...])
    lr = jnp.minimum(are_ref[...], -1e-4)
    li = aim_ref[...]
    mag = jnp.exp(lr * dt)
    ab_re = mag * jnp.cos(li * dt)
    ab_im = mag * jnp.sin(li * dt)
    den = lr * lr + li * li
    nr = ab_re - 1.0
    cr_ref[...] = (nr * lr + ab_im * li) / den
    ci_ref[...] = (ab_im * lr - nr * li) / den
    p_re, p_im = ab_re, ab_im
    for k in range(SCAN_ROWS):
        abre_ref[k] = p_re
        abim_ref[k] = p_im
        p_re, p_im = p_re * ab_re - p_im * ab_im, p_re * ab_im + p_im * ab_re


def _s5_prep(a_re, a_im, log_dt):
    g, n = a_re.shape
    full = pl.BlockSpec((g, n), lambda: (0, 0))
    pw = pl.BlockSpec((SCAN_ROWS, g, n), lambda: (0, 0, 0))
    return pl.pallas_call(
        _s5_prep_body,
        in_specs=[full, full, pl.BlockSpec((g, 1), lambda: (0, 0))],
        out_specs=[pw, pw, full, full],
        out_shape=[jax.ShapeDtypeStruct((SCAN_ROWS, g, n), F32)] * 2 + [jax.ShapeDtypeStruct((g, n), F32)] * 2,
    )(a_re, a_im, log_dt.reshape(g, 1))


def _s5_body(u_ref, bre_ref, bim_ref, cre_ref, cim_ref, pre_ref, pim_ref, cr_ref, ci_ref, d_ref, o_ref,
             hre_s, him_s):
    s = u_ref.shape[1]
    rt = _row_tile(s)
    cr = cr_ref[...]
    ci = ci_ref[...]
    b_re = _split_bf16(bre_ref[0])
    b_im = _split_bf16(bim_ref[0])
    for r in range(0, s, rt):
        u = u_ref[0, r:r + rt, :]
        xr = _dot_split(u, *b_re)
        xi = _dot_split(u, *b_im)
        hre_s[r:r + rt, :] = cr * xr - ci * xi
        him_s[r:r + rt, :] = cr * xi + ci * xr
    p_re = pre_ref[...]
    p_im = pim_ref[...]

    def step(i, carry):
        c_re, c_im = carry
        r0 = pl.multiple_of(i * SCAN_ROWS, SCAN_ROWS)
        x_re = hre_s[pl.ds(r0, SCAN_ROWS), :]
        x_im = him_s[pl.ds(r0, SCAN_ROWS), :]
        for d in SCAN_SHIFTS:
            a_re = p_re[d - 1:d, :]
            a_im = p_im[d - 1:d, :]
            s_re = _shift_rows(x_re, d, 0.0)
            s_im = _shift_rows(x_im, d, 0.0)
            x_re, x_im = x_re + a_re * s_re - a_im * s_im, x_im + a_re * s_im + a_im * s_re
        h_re = x_re + p_re * c_re - p_im * c_im
        h_im = x_im + p_re * c_im + p_im * c_re
        hre_s[pl.ds(r0, SCAN_ROWS), :] = h_re
        him_s[pl.ds(r0, SCAN_ROWS), :] = h_im
        return h_re[SCAN_ROWS - 1:SCAN_ROWS, :], h_im[SCAN_ROWS - 1:SCAN_ROWS, :]

    zero = jnp.zeros((1, hre_s.shape[1]), F32)
    lax.fori_loop(0, s // SCAN_ROWS, step, (zero, zero))
    dvec = d_ref[...]
    cmat_re = _split_bf16(cre_ref[0])
    cmat_im = _split_bf16(cim_ref[0])
    for r in range(0, s, rt):
        y = _dot_split(hre_s[r:r + rt, :], *cmat_re) - _dot_split(him_s[r:r + rt, :], *cmat_im)
        y = y + dvec * u_ref[0, r:r + rt, :]
        o_ref[0, r:r + rt, :] = _gelu(y)


def _block_diag(x, nblk):
    g = x.shape[0] // nblk
    r, c = x.shape[1:]
    eye = jnp.eye(g, dtype=x.dtype)
    return jnp.einsum("jgrc,gh->jgrhc", x.reshape(nblk, g, r, c), eye).reshape(nblk, g * r, g * c)


def _s5(proj, col0, a_re, a_im, log_dt, b_re, b_im, c_re, c_im, dvec):
    b, s, _ = proj.shape
    g, n = a_re.shape
    width = g * S5_GROUP
    gpb = LANES // S5_GROUP
    nblk = g // gpb
    nst = gpb * n
    p_re, p_im, cr, ci = _s5_prep(a_re, a_im, log_dt)
    bmat_re = _block_diag(jnp.transpose(b_re, (0, 2, 1)), nblk)
    bmat_im = _block_diag(jnp.transpose(b_im, (0, 2, 1)), nblk)
    cmat_re = _block_diag(jnp.transpose(c_re, (0, 2, 1)), nblk)
    cmat_im = _block_diag(jnp.transpose(c_im, (0, 2, 1)), nblk)
    ublk = col0 // LANES
    st_row = lambda rows: pl.BlockSpec((rows, nst), lambda i, j: (0, j))
    return pl.pallas_call(
        _s5_body,
        grid=(b, nblk),
        in_specs=[pl.BlockSpec((1, s, LANES), lambda i, j: (i, 0, ublk + j)),
                  pl.BlockSpec((1, LANES, nst), lambda i, j: (j, 0, 0)),
                  pl.BlockSpec((1, LANES, nst), lambda i, j: (j, 0, 0)),
                  pl.BlockSpec((1, nst, LANES), lambda i, j: (j, 0, 0)),
                  pl.BlockSpec((1, nst, LANES), lambda i, j: (j, 0, 0)),
                  st_row(SCAN_ROWS), st_row(SCAN_ROWS), st_row(1), st_row(1),
                  pl.BlockSpec((1, LANES), lambda i, j: (0, j))],
        out_specs=pl.BlockSpec((1, s, LANES), lambda i, j: (i, 0, j)),
        out_shape=jax.ShapeDtypeStruct((b, s, width), F32),
        scratch_shapes=[pltpu.VMEM((s, nst), F32), pltpu.VMEM((s, nst), F32)],
        compiler_params=_params(("parallel", "parallel")),
    )(proj, bmat_re, bmat_im, cmat_re, cmat_im, p_re.reshape(SCAN_ROWS, g * n), p_im.reshape(SCAN_ROWS, g * n),
      cr.reshape(1, g * n), ci.reshape(1, g * n), dvec.reshape(1, width))


def _glu_body(y_ref, w_ref, b_ref, o_ref):
    y = y_ref[...]
    o_ref[...] = (y * _sigmoid(_dot(y.astype(BF16), w_ref[...]) + b_ref[...])).astype(o_ref.dtype)


def _glu(y, w, bias, tm=512):
    t, d = y.shape
    tm = min(tm, t)
    return pl.pallas_call(
        _glu_body,
        grid=(t // tm,),
        in_specs=[pl.BlockSpec((tm, d), lambda i: (i, 0)), pl.BlockSpec((d, d), lambda i: (0, 0)),
                  pl.BlockSpec((1, d), lambda i: (0, 0))],
        out_specs=pl.BlockSpec((tm, d), lambda i: (i, 0)),
        out_shape=jax.ShapeDtypeStruct((t, d), BF16),
        compiler_params=_params(("parallel",)),
    )(y, w.astype(BF16), bias.reshape(1, d))


PEER_RANK_NONE = 127.0


def _top_rows(s, k, with_rank=False):
    rows = []
    rank = jnp.full(s.shape, PEER_RANK_NONE, F32) if with_rank else None
    for i in range(k):
        m = jnp.max(s, axis=0, keepdims=True)
        rows.append(m)
        hit = s == m
        if with_rank:
            rank = jnp.where(hit, float(i), rank)
        s = jnp.where(hit, NEG_BIG, s)
    return (rows, rank) if with_rank else rows


def _peer_route_lanes(s1, s2):
    nk = PEER_TOPK + 1
    v1 = _top_rows(s1, nk)
    v2_rows, rank2 = _top_rows(s2, nk, with_rank=True)
    pad = jnp.full((-nk % SUBLANES, s1.shape[1]), NEG_BIG, F32)
    v2 = jnp.concatenate(v2_rows + [pad], axis=0)
    cand = [v1[0] + v2]
    for i in range(1, nk):
        need = nk // (i + 1)
        cand.append(v1[i] + v2[:-(-need // SUBLANES) * SUBLANES, :])
    top = _top_rows(jnp.concatenate(cand, axis=0), nk)
    zsum = jnp.ones_like(top[0])
    for r in top[1:PEER_TOPK]:
        zsum = zsum + jnp.exp(r - top[0])
    d1 = 0.5 * (top[PEER_TOPK - 1] + top[PEER_TOPK]) - s1
    cnt = jnp.zeros_like(s1)
    for r in v2_rows:
        cnt = cnt + jnp.where(r >= d1, 1.0, 0.0)
    e1 = jnp.exp(s1 - v1[0])
    e2 = jnp.exp(s2 - v2[0:1, :]) * (0.5 / zsum)
    return cnt, e1, rank2.astype(BF16), e2.astype(BF16)


def _peer_route_body(q_ref, keys_ref, cnt1_ref, e1_ref, rank2_ref, e2_ref):
    half = keys_ref.shape[3]
    q = q_ref[...]
    s1 = _dot_nt(keys_ref[0, 0], q[:, :half], HIGHEST)
    s2 = _dot_nt(keys_ref[0, 1], q[:, half:], HIGHEST)
    for c in range(0, q.shape[0], LANES):
        tok = slice(c, c + LANES)
        cnt, e1, rank2, e2 = _peer_route_lanes(s1[:, tok], s2[:, tok])
        cnt1_ref[0, :, tok] = cnt
        e1_ref[0, :, tok] = e1
        rank2_ref[0, :, tok] = rank2
        e2_ref[0, :, tok] = e2


def _peer_route(q, keys, tt=512):
    t = q.shape[0]
    heads, _, nkeys, half = keys.shape
    tt = min(tt, t)
    table = pl.BlockSpec((1, nkeys, tt), lambda i, h: (h, 0, i))
    f32_shape = jax.ShapeDtypeStruct((heads, nkeys, t), F32)
    b16_shape = jax.ShapeDtypeStruct((heads, nkeys, t), BF16)
    return pl.pallas_call(
        _peer_route_body,
        grid=(t // tt, heads),
        in_specs=[pl.BlockSpec((tt, 2 * half), lambda i, h: (i, h)),
                  pl.BlockSpec((1, 2, nkeys, half), lambda i, h: (h, 0, 0, 0))],
        out_specs=[table] * 4,
        out_shape=[f32_shape, f32_shape, b16_shape, b16_shape],
        compiler_params=_params(("parallel", "parallel")),
    )(q, keys)


def _rows_bf16(row, n):
    one = jnp.broadcast_to(row, (BF16_ROWS, row.shape[1])).astype(BF16)
    return jnp.concatenate([one] * (n // BF16_ROWS), axis=0)


def _peer_dense_body(x_ref, u_ref, v_ref, cnt1_ref, e1_ref, rank2_ref, e2_ref, o_ref):
    heads, nkeys, _ = rank2_ref.shape
    te = u_ref.shape[0]
    j = pl.program_id(1)

    @pl.when(j == 0)
    def _():
        o_ref[...] = jnp.zeros_like(o_ref)

    pre = _dot_nt(u_ref[...], x_ref[...])
    act = pre * (1.0 + jnp.tanh(pre * (0.7978845608028654 + 0.035677408136300125 * (pre * pre))))
    parts = []
    for r in range(te // nkeys):
        i1 = j * (te // nkeys) + r
        w = None
        for h in range(heads):
            cnt = _rows_bf16(cnt1_ref[h, pl.ds(i1, 1), :], nkeys)
            e1 = _rows_bf16(e1_ref[h, pl.ds(i1, 1), :], nkeys)
            term = jnp.where(rank2_ref[h] < cnt, e2_ref[h] * e1, jnp.zeros((), BF16))
            w = term if w is None else w + term
        parts.append(w.astype(F32) * act[r * nkeys:(r + 1) * nkeys, :])
    p_t = jnp.concatenate(parts, axis=0) if len(parts) > 1 else parts[0]
    o_ref[...] += _dot(p_t.T.astype(BF16), v_ref[...])


def _peer_dense(x16, u16, v16, tables, tt=512, te=512):
    t, d = x16.shape
    e = u16.shape[0]
    heads, nkeys, _ = tables[0].shape
    tt = min(tt, t)
    te = min(te, e)
    assert te % nkeys == 0 and e == nkeys * nkeys
    table = pl.BlockSpec((heads, nkeys, tt), lambda i, j: (0, 0, i))
    return pl.pallas_call(
        _peer_dense_body,
        grid=(t // tt, e // te),
        in_specs=[pl.BlockSpec((tt, d), lambda i, j: (i, 0)),
                  pl.BlockSpec((te, d), lambda i, j: (j, 0)),
                  pl.BlockSpec((te, d), lambda i, j: (j, 0)),
                  table, table, table, table],
        out_specs=pl.BlockSpec((tt, d), lambda i, j: (i, 0)),
        out_shape=jax.ShapeDtypeStruct((t, d), F32),
        compiler_params=_params(("parallel", "arbitrary")),
    )(x16, u16, v16, *tables)


def _peer_ffn(h16, wq16, keys, u16, v16):
    q = _matmul(h16, wq16)
    return _peer_dense(h16, u16, v16, _peer_route(q, keys))


def _even_mixer(x2d, b, s, j, w_in_all, gdn_conv_w, a_log, dt_bias, norm_w, rg_conv_w, rg_conv_b,
                rg_wa, rg_ba, rg_wx, rg_bx, rg_lambda, w_out_all):
    heads = a_log.shape[0]
    gw = heads * HEAD_DIM
    rw = rg_lambda.shape[0]
    ab_pad = LANES - 2 * heads
    w_t = _stage_bf16(jnp.swapaxes(w_in_all, 1, 2), j)
    w_ab_t = jnp.pad(w_t[4 * gw:4 * gw + 2 * heads], ((0, ab_pad), (0, 0)))
    qkvz = _matmul_nt(x2d, w_t, n=4 * gw).reshape(b, s, 4 * gw)
    ab = _matmul_nt(x2d, w_ab_t).reshape(b, s, LANES)
    rg = _matmul_nt(x2d, w_t[4 * gw + 2 * heads:]).reshape(b, s, 2 * rw)
    gdn_out = _gdn(qkvz, ab, gdn_conv_w, a_log, dt_bias, norm_w, heads)
    rg_out = _rg_lru(rg, rg_conv_w, rg_conv_b, rg_wa, rg_wx, rg_ba, rg_bx, rg_lambda)
    return _matmul_pair(gdn_out.reshape(b * s, gw), rg_out.reshape(b * s, rw), _stage_bf16(w_out_all, j), out_dtype=BF16)


def _odd_mixer(x2d, b, s, j, w_in_all, a_re, a_im, log_dt, b_re, b_im, c_re, c_im, dvec, glu_w, glu_b,
               w_out_all):
    sw = dvec.shape[0]
    sbw = (w_in_all.shape[2] - sw) // 3
    heads = sbw // HEAD_DIM
    proj = _matmul(x2d, _stage_bf16(w_in_all, j)).reshape(b, s, 3 * sbw + sw)
    sb_out = _stick_breaking(proj, heads)
    yg = _s5(proj, 3 * sbw, a_re, a_im, log_dt, b_re, b_im, c_re, c_im, dvec)
    s5_out = _glu(yg.reshape(b * s, sw), glu_w, glu_b)
    return _matmul_pair(sb_out.reshape(b * s, sbw), s5_out, _stage_bf16(w_out_all, j), out_dtype=BF16)


def kernel(x, w_in_e, gdn_conv_w, gdn_A_log, gdn_dt_bias, gdn_norm_w, rg_conv_w, rg_conv_b, rg_wa, rg_ba,
           rg_wx, rg_bx, rg_lambda, w_out_e, w_in_o, s5_A_re, s5_A_im, s5_log_dt, s5_B_re, s5_B_im,
           s5_C_re, s5_C_im, s5_D, s5_glu_w, s5_glu_b, w_out_o, ln_mix_g, ln_mix_b, peer_wq, peer_keys,
           peer_u, peer_v, ln_ffn_g, ln_ffn_b):
    b, s, d = x.shape
    depth = ln_mix_g.shape[0]
    h = x.reshape(b * s, d)
    h16 = h.astype(BF16)
    for layer in range(depth):
        j = layer // 2
        if layer % 2 == 0:
            mix = _even_mixer(h16, b, s, j, w_in_e, gdn_conv_w[j], gdn_A_log[j], gdn_dt_bias[j], gdn_norm_w[j],
                              rg_conv_w[j], rg_conv_b[j], rg_wa[j], rg_ba[j], rg_wx[j], rg_bx[j],
                              rg_lambda[j], w_out_e)
        else:
            mix = _odd_mixer(h16, b, s, j, w_in_o, s5_A_re[j], s5_A_im[j], s5_log_dt[j], s5_B_re[j],
                             s5_B_im[j], s5_C_re[j], s5_C_im[j], s5_D[j], s5_glu_w[j], s5_glu_b[j], w_out_o)
        h, h16 = _residual_ln(h, mix, ln_mix_g[layer], ln_mix_b[layer])
        ffn = _peer_ffn(h16, _stage_bf16(peer_wq, layer), peer_keys[layer],
                        _stage_bf16(peer_u, layer), _stage_bf16(peer_v, layer))
        h, h16 = _residual_ln(h, ffn, ln_ffn_g[layer], ln_ffn_b[layer])
    return h.reshape(b, s, d)
```

```python
import functools
import math

import jax
import jax.numpy as jnp
from jax import lax
from jax.experimental import pallas as pl
from jax.experimental.pallas import tpu as pltpu

F32 = jnp.float32
BF16 = jnp.bfloat16
HIGHEST = lax.Precision.HIGHEST

LANES = 128
SUBLANES = 8
BF16_ROWS = 2 * SUBLANES
VMEM_BYTES_V7X = 64 * 1024 * 1024
VMEM_LIMIT = VMEM_BYTES_V7X - 8 * 1024 * 1024

HEAD_DIM = 128
CONV_W = 4
CONV_PAD = SUBLANES
GDN_CHUNK = 64
RG_BLOCK = 128
LRU_C = 8.0
S5_GROUP = 16
S5_STATE = 64
PEER_TOPK = 16
DEPTH = 2
DN_ALPHA = (2.0 * DEPTH) ** 0.25
LN_EPS = 1e-5
RMS_EPS = 1e-6
NEG_BIG = -3.0e38


def _params(sem, vmem=VMEM_LIMIT):
    return pltpu.CompilerParams(dimension_semantics=sem, vmem_limit_bytes=vmem)


def _softplus(x):
    return jnp.maximum(x, 0.0) + jnp.log1p(jnp.exp(-jnp.abs(x)))


def _sigmoid(x):
    return 1.0 / (1.0 + jnp.exp(-x))


def _gelu(x):
    return 0.5 * x * (1.0 + jnp.tanh(0.7978845608028654 * (x + 0.044715 * (x * x * x))))


def _dot(a, b, precision=None):
    return jnp.dot(a, b, preferred_element_type=F32, precision=precision)


def _split_bf16(x):
    hi = x.astype(BF16)
    return hi, (x - hi.astype(F32)).astype(BF16)


def _dot_split(a, b_hi, b_lo):
    a_hi, a_lo = _split_bf16(a)
    return _dot(a_hi, b_hi) + (_dot(a_lo, b_hi) + _dot(a_hi, b_lo))


def _dot_nt(a, b, precision=None):
    return lax.dot_general(a, b, (((1,), (1,)), ((), ())), preferred_element_type=F32, precision=precision)


def _mm_body(a_ref, b_ref, o_ref, acc_ref):
    k = pl.program_id(2)

    @pl.when(k == 0)
    def _():
        acc_ref[...] = jnp.zeros_like(acc_ref)

    acc_ref[...] += _dot(a_ref[...].astype(BF16), b_ref[...])

    @pl.when(k == pl.num_programs(2) - 1)
    def _():
        o_ref[...] = acc_ref[...].astype(o_ref.dtype)


def _mm_single_body(a_ref, b_ref, o_ref):
    o_ref[...] = _dot(a_ref[...].astype(BF16), b_ref[...]).astype(o_ref.dtype)


def _matmul(a, b, out_dtype=F32, tm=1024, tn=1024, tk=1024):
    m, kd = a.shape
    _, n = b.shape
    tm, tn = min(tm, m), min(tn, n)
    assert m % tm == 0 and n % tn == 0, (a.shape, b.shape)
    out_bytes = jnp.dtype(out_dtype).itemsize
    tile_bytes = 2 * (tm * kd * a.dtype.itemsize + kd * tn * b.dtype.itemsize + tm * tn * out_bytes)
    if tile_bytes <= (VMEM_LIMIT * 3) // 4:
        return pl.pallas_call(
            _mm_single_body,
            grid=(m // tm, n // tn),
            in_specs=[pl.BlockSpec((tm, kd), lambda i, j: (i, 0)),
                      pl.BlockSpec((kd, tn), lambda i, j: (0, j))],
            out_specs=pl.BlockSpec((tm, tn), lambda i, j: (i, j)),
            out_shape=jax.ShapeDtypeStruct((m, n), out_dtype),
            compiler_params=_params(("parallel", "parallel")),
        )(a, b)
    tk = min(tk, kd)
    assert kd % tk == 0, (a.shape, b.shape)
    return pl.pallas_call(
        _mm_body,
        grid=(m // tm, n // tn, kd // tk),
        in_specs=[pl.BlockSpec((tm, tk), lambda i, j, k: (i, k)),
                  pl.BlockSpec((tk, tn), lambda i, j, k: (k, j))],
        out_specs=pl.BlockSpec((tm, tn), lambda i, j, k: (i, j)),
        out_shape=jax.ShapeDtypeStruct((m, n), out_dtype),
        scratch_shapes=[pltpu.VMEM((tm, tn), F32)],
        compiler_params=_params(("parallel", "parallel", "arbitrary")),
    )(a, b)


def _mm_nt_body(a_ref, bt_ref, o_ref):
    o_ref[...] = _dot_nt(a_ref[...], bt_ref[...]).astype(o_ref.dtype)


def _matmul_nt(a, bt, n=None, out_dtype=F32, tm=1024, tn=1024):
    m, kd = a.shape
    n = bt.shape[0] if n is None else n
    tm, tn = min(tm, m), min(tn, n)
    assert m % tm == 0 and n % tn == 0 and bt.shape[1] == kd
    return pl.pallas_call(
        _mm_nt_body,
        grid=(m // tm, n // tn),
        in_specs=[pl.BlockSpec((tm, kd), lambda i, j: (i, 0)),
                  pl.BlockSpec((tn, kd), lambda i, j: (j, 0))],
        out_specs=pl.BlockSpec((tm, tn), lambda i, j: (i, j)),
        out_shape=jax.ShapeDtypeStruct((m, n), out_dtype),
        compiler_params=_params(("parallel", "parallel")),
    )(a, bt)


def _mm_pair_body(a1_ref, a2_ref, b1_ref, b2_ref, o_ref):
    o_ref[...] = (_dot(a1_ref[...], b1_ref[...]) + _dot(a2_ref[...], b2_ref[...])).astype(o_ref.dtype)


def _matmul_pair(a1, a2, b, out_dtype=F32, tm=1024, tn=1024):
    m, k1 = a1.shape
    _, k2 = a2.shape
    _, n = b.shape
    tm, tn = min(tm, m), min(tn, n)
    assert m % tm == 0 and n % tn == 0 and k1 % k2 == 0 and b.shape[0] == k1 + k2
    return pl.pallas_call(
        _mm_pair_body,
        grid=(m // tm, n // tn),
        in_specs=[pl.BlockSpec((tm, k1), lambda i, j: (i, 0)),
                  pl.BlockSpec((tm, k2), lambda i, j: (i, 0)),
                  pl.BlockSpec((k1, tn), lambda i, j: (0, j)),
                  pl.BlockSpec((k2, tn), lambda i, j: (k1 // k2, j))],
        out_specs=pl.BlockSpec((tm, tn), lambda i, j: (i, j)),
        out_shape=jax.ShapeDtypeStruct((m, n), out_dtype),
        compiler_params=_params(("parallel", "parallel")),
    )(a1, a2, b, b)


def _cast_body(x_ref, o_ref):
    o_ref[...] = x_ref[0].astype(o_ref.dtype)


def _stage_bf16(x, layer, tile_bytes=8 * 1024 * 1024):
    _, r, cols = x.shape
    assert cols % LANES == 0
    limit = tile_bytes // (cols * x.dtype.itemsize)
    tr = max([t for t in range(BF16_ROWS, min(r, limit) + 1, BF16_ROWS) if r % t == 0], default=r)
    return pl.pallas_call(
        _cast_body,
        grid=(r // tr,),
        in_specs=[pl.BlockSpec((1, tr, cols), lambda i: (layer, i, 0))],
        out_specs=pl.BlockSpec((tr, cols), lambda i: (i, 0)),
        out_shape=jax.ShapeDtypeStruct((r, cols), BF16),
        compiler_params=_params(("parallel",)),
    )(x)


def _ln_body(h_ref, d_ref, g_ref, b_ref, o_ref, ob_ref):
    y = DN_ALPHA * h_ref[...] + d_ref[...].astype(F32)
    mu = jnp.mean(y, axis=-1, keepdims=True)
    yc = y - mu
    var = jnp.mean(yc * yc, axis=-1, keepdims=True)
    out = yc * lax.rsqrt(var + LN_EPS) * g_ref[...] + b_ref[...]
    o_ref[...] = out
    ob_ref[...] = out.astype(BF16)


def _residual_ln(h, delta, g, b, tm=256):
    t, d = h.shape
    tm = min(tm, t)
    row = pl.BlockSpec((tm, d), lambda i: (i, 0))
    vec = pl.BlockSpec((1, d), lambda i: (0, 0))
    return pl.pallas_call(
        _ln_body,
        grid=(t // tm,),
        in_specs=[row, row, vec, vec],
        out_specs=[row, row],
        out_shape=[jax.ShapeDtypeStruct((t, d), F32), jax.ShapeDtypeStruct((t, d), BF16)],
        compiler_params=_params(("parallel",)),
    )(h, delta, g.reshape(1, d), b.reshape(1, d))


def _conv_rows(xpad, w, r, rows):
    acc = None
    for k in range(CONV_W):
        tap = xpad[pl.ds(r + CONV_PAD - (CONV_W - 1) + k, rows), :] * w[k:k + 1, :]
        acc = tap if acc is None else acc + tap
    return acc


def _row_tile(s):
    return min(s, 256)


def _bmm(a, b):
    return jnp.einsum("gmk,gkn->gmn", a.astype(BF16), b.astype(BF16), preferred_element_type=F32)


def _bmm_nt(a, b):
    return jnp.einsum("gmk,gnk->gmn", a.astype(BF16), b.astype(BF16), preferred_element_type=F32)


def _gdn_body(q_ref, k_ref, v_ref, z_ref, wq_ref, wk_ref, wv_ref, arow_ref, brow_ref,
              alog_ref, dtb_ref, nw_ref, o_ref, xq, xk, xv, state):
    ts = q_ref.shape[1]
    hb = q_ref.shape[2] // HEAD_DIM
    c = GDN_CHUNK
    nc = ts // c
    g = hb * nc

    @pl.when(pl.program_id(2) == 0)
    def _():
        for xp in (xq, xk, xv):
            xp[0:CONV_PAD, :] = jnp.zeros((CONV_PAD, hb * HEAD_DIM), F32)
        state[...] = jnp.zeros_like(state)

    def conv_act(x_ref, xp, w_ref, kind):
        xp[CONV_PAD:CONV_PAD + ts, :] = x_ref[0]
        w = w_ref[...]
        heads = []
        for h in range(hb):
            sl = slice(h * HEAD_DIM, (h + 1) * HEAD_DIM)
            y = None
            for k in range(CONV_W):
                tap = xp[pl.ds(CONV_PAD - (CONV_W - 1) + k, ts), sl] * w[k:k + 1, sl]
                y = tap if y is None else y + tap
            y = y * _sigmoid(y)
            if kind != "v":
                y = y * lax.rsqrt(jnp.sum(y * y, axis=-1, keepdims=True) + RMS_EPS)
            if kind == "q":
                y = y * (HEAD_DIM ** -0.5)
            heads.append(y.reshape(nc, c, HEAD_DIM))
        xp[0:CONV_PAD, :] = xp[ts:ts + CONV_PAD, :]
        return jnp.concatenate(heads, axis=0)

    qc = conv_act(q_ref, xq, wq_ref, "q")
    kc = conv_act(k_ref, xk, wk_ref, "k")
    vc = conv_act(v_ref, xv, wv_ref, "v")

    neg_a = -jnp.exp(alog_ref[...])[:, None]
    dtb = dtb_ref[...][:, None]
    grow = (neg_a * _softplus(arow_ref[0] + dtb)).reshape(g, 1, c)
    brow = _sigmoid(brow_ref[0]).reshape(g, 1, c)

    ri = lax.broadcasted_iota(jnp.int32, (c, c), 0)
    ci = lax.broadcasted_iota(jnp.int32, (c, c), 1)
    lower = ri >= ci
    diag = ri == ci
    gcol = jnp.sum(jnp.where(diag, grow, 0.0), axis=2, keepdims=True)
    beta = jnp.sum(jnp.where(diag, brow, 0.0), axis=2, keepdims=True)
    gc_col = jnp.sum(jnp.where(lower, grow, 0.0), axis=2, keepdims=True)
    gc_row = jnp.sum(jnp.where(ri <= ci, gcol, 0.0), axis=1, keepdims=True)
    gamma = jnp.where(lower, jnp.exp(jnp.where(lower, gc_col - gc_row, 0.0)), 0.0)
    kb = kc * beta
    p = jnp.where(ri > ci, -(_bmm_nt(kb, kc) * gamma), 0.0)
    r = p
    for _ in range(int(math.log2(c)) - 1):
        p = _bmm(p, p)
        r = r + p + _bmm(r, p)
    eg = jnp.exp(gc_col)
    vb = vc * beta
    kbe = kb * eg
    u = vb + _bmm(r, vb)
    w = kbe + _bmm(r, kbe)
    a_qk = _bmm_nt(qc, kc) * gamma
    g_last = gc_row[:, :, c - 1:c]
    k_tail = kc * jnp.exp(g_last - gc_col)
    q_dec = qc * eg
    decay = jnp.exp(g_last)

    per_head = lambda x: x.reshape((hb, nc) + x.shape[1:])
    u, w, a_qk, k_tail, q_dec, decay = [per_head(x) for x in (u, w, a_qk, k_tail, q_dec, decay)]
    nw = nw_ref[...]
    st = state[...]
    for i in range(nc):
        v_new = u[:, i] - _bmm(w[:, i], st)
        o = _bmm(q_dec[:, i], st) + _bmm(a_qk[:, i], v_new)
        kt = jnp.swapaxes(k_tail[:, i], 1, 2)
        st = st * decay[:, i] + _bmm(kt, v_new)
        o = o * lax.rsqrt(jnp.mean(o * o, axis=-1, keepdims=True) + RMS_EPS) * nw
        for h in range(hb):
            rows = slice(i * c, (i + 1) * c)
            cols = slice(h * HEAD_DIM, (h + 1) * HEAD_DIM)
            zz = z_ref[0, rows, cols]
            o_ref[0, rows, cols] = (o[h] * (zz * _sigmoid(zz))).astype(o_ref.dtype)
    state[...] = st


def _gdn(qkvz, ab, conv_w, a_log, dt_bias, norm_w, heads, heads_per_step=8, ts=256):
    b, s, _ = qkvz.shape
    c = GDN_CHUNK
    hb = min(heads_per_step, heads)
    ts = min(ts, s)
    ng = heads // hb
    rows = lambda x: jnp.transpose(x, (0, 2, 1)).reshape(b, heads, s // c, 1, c)
    arow = rows(ab[:, :, :heads])
    brow = rows(ab[:, :, heads:2 * heads])
    wide = hb * HEAD_DIM
    head = lambda off: pl.BlockSpec((1, ts, wide), lambda i, j, t: (i, t, off + j))
    cw = lambda off: pl.BlockSpec((CONV_W, wide), lambda i, j, t: (0, off + j))
    row = pl.BlockSpec((1, hb, ts // c, 1, c), lambda i, j, t: (i, j, t, 0, 0))
    scal = pl.BlockSpec((hb, 1, 1), lambda i, j, t: (j, 0, 0))
    return pl.pallas_call(
        _gdn_body,
        grid=(b, ng, s // ts),
        in_specs=[head(0), head(ng), head(2 * ng), head(3 * ng),
                  cw(0), cw(ng), cw(2 * ng), row, row,
                  scal, scal, pl.BlockSpec((1, HEAD_DIM), lambda i, j, t: (0, 0))],
        out_specs=pl.BlockSpec((1, ts, wide), lambda i, j, t: (i, t, j)),
        out_shape=jax.ShapeDtypeStruct((b, s, heads * HEAD_DIM), BF16),
        scratch_shapes=[pltpu.VMEM((ts + CONV_PAD, wide), F32)] * 3 + [pltpu.VMEM((hb, HEAD_DIM, HEAD_DIM), F32)],
        compiler_params=_params(("parallel", "parallel", "arbitrary")),
    )(qkvz, qkvz, qkvz, qkvz, conv_w, conv_w, conv_w, arow, brow,
      a_log.reshape(heads, 1, 1), dt_bias.reshape(heads, 1, 1), norm_w.reshape(1, HEAD_DIM))


SCAN_ROWS = SUBLANES
SCAN_SHIFTS = tuple(1 << i for i in range(int(math.log2(SCAN_ROWS))))


def _shift_rows(x, d, fill):
    row = lax.broadcasted_iota(jnp.int32, x.shape, 0)
    return jnp.where(row >= d, pltpu.roll(x, d, 0), fill)


def _rg_body(x_ref, gate_ref, cw_ref, cb_ref, wa_ref, wx_ref, ba_ref, bx_ref, lam_ref, o_ref,
             xpad, a_s, h_s):
    s = x_ref.shape[1]
    cb = x_ref.shape[2]
    nb = cb // RG_BLOCK
    rt = _row_tile(s)
    xpad[0:CONV_PAD, :] = jnp.zeros((CONV_PAD, cb), F32)
    xpad[CONV_PAD:CONV_PAD + s, :] = x_ref[0]
    w = cw_ref[...]
    bias = cb_ref[...]
    lam_sp = _softplus(-lam_ref[...])
    for r in range(0, s, rt):
        xr = _conv_rows(xpad, w, r, rt) + bias
        for n in range(nb):
            sl = slice(n * RG_BLOCK, (n + 1) * RG_BLOCK)
            xb = xr[:, sl]
            xb16 = xb.astype(BF16)
            rgate = _sigmoid(_dot(xb16, wa_ref[n]) + ba_ref[:, sl])
            igate = _sigmoid(_dot(xb16, wx_ref[n]) + bx_ref[:, sl])
            log_a = -LRU_C * rgate * lam_sp[:, sl]
            a = jnp.exp(log_a)
            a_s[r:r + rt, sl] = a
            one_minus_a2 = jnp.tanh(-log_a) * (a * a + 1.0)
            h_s[r:r + rt, sl] = jnp.sqrt(jnp.maximum(one_minus_a2, 0.0)) * (igate * xb)

    def step(i, carry):
        r0 = pl.multiple_of(i * SCAN_ROWS, SCAN_ROWS)
        a = a_s[pl.ds(r0, SCAN_ROWS), :]
        bv = h_s[pl.ds(r0, SCAN_ROWS), :]
        for d in SCAN_SHIFTS:
            bv = a * _shift_rows(bv, d, 0.0) + bv
            a = a * _shift_rows(a, d, 1.0)
        h = bv + a * carry
        h_s[pl.ds(r0, SCAN_ROWS), :] = h
        return h[SCAN_ROWS - 1:SCAN_ROWS, :]

    lax.fori_loop(0, s // SCAN_ROWS, step, jnp.zeros((1, cb), F32))
    for r in range(0, s, rt):
        o_ref[0, r:r + rt, :] = (h_s[r:r + rt, :] * _gelu(gate_ref[0, r:r + rt, :])).astype(o_ref.dtype)


def _rg_lru(rg, conv_w, conv_b, wa, wx, ba, bx, lam, blocks_per_step=4):
    b, s, w2 = rg.shape
    width = w2 // 2
    nblk = width // RG_BLOCK
    nb = min(blocks_per_step, nblk)
    cb = nb * RG_BLOCK
    steps = width // cb
    vec = pl.BlockSpec((1, cb), lambda i, j: (0, j))
    wblk = pl.BlockSpec((nb, RG_BLOCK, RG_BLOCK), lambda i, j: (j, 0, 0))
    return pl.pallas_call(
        _rg_body,
        grid=(b, steps),
        in_specs=[pl.BlockSpec((1, s, cb), lambda i, j: (i, 0, j)),
                  pl.BlockSpec((1, s, cb), lambda i, j: (i, 0, steps + j)),
                  pl.BlockSpec((CONV_W, cb), lambda i, j: (0, j)), vec, wblk, wblk, vec, vec, vec],
        out_specs=pl.BlockSpec((1, s, cb), lambda i, j: (i, 0, j)),
        out_shape=jax.ShapeDtypeStruct((b, s, width), BF16),
        scratch_shapes=[pltpu.VMEM((s + CONV_PAD, cb), F32), pltpu.VMEM((s, cb), F32), pltpu.VMEM((s, cb), F32)],
        compiler_params=_params(("parallel", "parallel")),
    )(rg, rg, conv_w, conv_b.reshape(1, width), wa.astype(BF16), wx.astype(BF16),
      ba.reshape(1, width), bx.reshape(1, width), lam.reshape(1, width))


SB_LOG_WEIGHT_FLOOR = -104.0


def _sb_body(q_ref, k_ref, v_ref, o_ref, *, bk):
    bq = q_ref.shape[1]
    hb = q_ref.shape[2] // HEAD_DIM
    qi = pl.program_id(2)
    t_pos = qi * bq + lax.broadcasted_iota(jnp.int32, (bq, bk), 0)
    s_off = lax.broadcasted_iota(jnp.int32, (bq, bk), 1)
    tri = (lax.broadcasted_iota(jnp.int32, (bk, bk), 0) > lax.broadcasted_iota(jnp.int32, (bk, bk), 1)).astype(BF16)
    nkb = (qi + 1) * (bq // bk)
    cols = [slice(h * HEAD_DIM, (h + 1) * HEAD_DIM) for h in range(hb)]
    q16 = [(q_ref[0, :, sl] * (HEAD_DIM ** -0.5)).astype(BF16) for sl in cols]

    def cond(carry):
        i, alive = carry[0], carry[1]
        return jnp.logical_and(i < nkb, alive)

    def body(carry):
        i, _, accs, runs = carry
        r0 = pl.multiple_of((nkb - 1 - i) * bk, bk)
        valid = (s_off + r0) < t_pos
        new_accs, new_runs = [], []
        top = None
        for h in range(hb):
            kblk = k_ref[0, pl.ds(r0, bk), cols[h]].astype(BF16)
            vblk = v_ref[0, pl.ds(r0, bk), cols[h]].astype(BF16)
            z = _dot_nt(q16[h], kblk)
            sp = _softplus(z)
            lm = jnp.where(valid, -sp, 0.0)
            hi = lm.astype(BF16)
            lo = (lm - hi.astype(F32)).astype(BF16)
            tail = _dot(hi, tri) + _dot(lo, tri) + runs[h]
            wts = jnp.where(valid, jnp.exp(z - sp + tail), 0.0)
            new_accs.append(accs[h] + _dot(wts.astype(BF16), vblk))
            run = runs[h] + jnp.sum(lm, axis=1, keepdims=True)
            new_runs.append(run)
            m = jnp.max(run)
            top = m if top is None else jnp.maximum(top, m)
        return i + 1, top > SB_LOG_WEIGHT_FLOOR, tuple(new_accs), tuple(new_runs)

    init = (jnp.int32(0), jnp.bool_(True),
            tuple(jnp.zeros((bq, HEAD_DIM), F32) for _ in range(hb)),
            tuple(jnp.zeros((bq, 1), F32) for _ in range(hb)))
    accs = lax.while_loop(cond, body, init)[2]
    for h in range(hb):
        o_ref[0, :, cols[h]] = accs[h].astype(o_ref.dtype)


def _stick_breaking(proj, heads, bq=256, bk=256, heads_per_step=2):
    b, s, _ = proj.shape
    bq = min(bq, s)
    bk = min(bk, bq)
    hb = min(heads_per_step, heads)
    ng = heads // hb
    wide = hb * HEAD_DIM
    return pl.pallas_call(
        functools.partial(_sb_body, bk=bk),
        grid=(b, ng, s // bq),
        in_specs=[pl.BlockSpec((1, bq, wide), lambda i, j, t: (i, t, j)),
                  pl.BlockSpec((1, s, wide), lambda i, j, t: (i, 0, ng + j)),
                  pl.BlockSpec((1, s, wide), lambda i, j, t: (i, 0, 2 * ng + j))],
        out_specs=pl.BlockSpec((1, bq, wide), lambda i, j, t: (i, t, j)),
        out_shape=jax.ShapeDtypeStruct((b, s, heads * HEAD_DIM), BF16),
        compiler_params=_params(("parallel", "parallel", "arbitrary")),
    )(proj, proj, proj)


def _s5_prep_body(are_ref, aim_ref, ldt_ref, abre_ref, abim_ref, cr_ref, ci_ref):
    dt = jnp.exp(ldt_ref[...])
    lr = jnp.minimum(are_ref[...], -1e-4)
    li = aim_ref[...]
    mag = jnp.exp(lr * dt)
    ab_re = mag * jnp.cos(li * dt)
    ab_im = mag * jnp.sin(li * dt)
    den = lr * lr + li * li
    nr = ab_re - 1.0
    cr_ref[...] = (nr * lr + ab_im * li) / den
    ci_ref[...] = (ab_im * lr - nr * li) / den
    p_re, p_im = ab_re, ab_im
    for k in range(SCAN_ROWS):
        abre_ref[k] = p_re
        abim_ref[k] = p_im
        p_re, p_im = p_re * ab_re - p_im * ab_im, p_re * ab_im + p_im * ab_re


def _s5_prep(a_re, a_im, log_dt):
    g, n = a_re.shape
    full = pl.BlockSpec((g, n), lambda: (0, 0))
    pw = pl.BlockSpec((SCAN_ROWS, g, n), lambda: (0, 0, 0))
    return pl.pallas_call(
        _s5_prep_body,
        in_specs=[full, full, pl.BlockSpec((g, 1), lambda: (0, 0))],
        out_specs=[pw, pw, full, full],
        out_shape=[jax.ShapeDtypeStruct((SCAN_ROWS, g, n), F32)] * 2 + [jax.ShapeDtypeStruct((g, n), F32)] * 2,
    )(a_re, a_im, log_dt.reshape(g, 1))


def _s5_body(u_ref, bre_ref, bim_ref, cre_ref, cim_ref, pre_ref, pim_ref, cr_ref, ci_ref, d_ref, o_ref,
             hre_s, him_s):
    s = u_ref.shape[1]
    rt = _row_tile(s)
    cr = cr_ref[...]
    ci = ci_ref[...]
    b_re = _split_bf16(bre_ref[0])
    b_im = _split_bf16(bim_ref[0])
    for r in range(0, s, rt):
        u = u_ref[0, r:r + rt, :]
        xr = _dot_split(u, *b_re)
        xi = _dot_split(u, *b_im)
        hre_s[r:r + rt, :] = cr * xr - ci * xi
        him_s[r:r + rt, :] = cr * xi + ci * xr
    p_re = pre_ref[...]
    p_im = pim_ref[...]

    def step(i, carry):
        c_re, c_im = carry
        r0 = pl.multiple_of(i * SCAN_ROWS, SCAN_ROWS)
        x_re = hre_s[pl.ds(r0, SCAN_ROWS), :]
        x_im = him_s[pl.ds(r0, SCAN_ROWS), :]
        for d in SCAN_SHIFTS:
            a_re = p_re[d - 1:d, :]
            a_im = p_im[d - 1:d, :]
            s_re = _shift_rows(x_re, d, 0.0)
            s_im = _shift_rows(x_im, d, 0.0)
            x_re, x_im = x_re + a_re * s_re - a_im * s_im, x_im + a_re * s_im + a_im * s_re
        h_re = x_re + p_re * c_re - p_im * c_im
        h_im = x_im + p_re * c_im + p_im * c_re
        hre_s[pl.ds(r0, SCAN_ROWS), :] = h_re
        him_s[pl.ds(r0, SCAN_ROWS), :] = h_im
        return h_re[SCAN_ROWS - 1:SCAN_ROWS, :], h_im[SCAN_ROWS - 1:SCAN_ROWS, :]

    zero = jnp.zeros((1, hre_s.shape[1]), F32)
    lax.fori_loop(0, s // SCAN_ROWS, step, (zero, zero))
    dvec = d_ref[...]
    cmat_re = _split_bf16(cre_ref[0])
    cmat_im = _split_bf16(cim_ref[0])
    for r in range(0, s, rt):
        y = _dot_split(hre_s[r:r + rt, :], *cmat_re) - _dot_split(him_s[r:r + rt, :], *cmat_im)
        y = y + dvec * u_ref[0, r:r + rt, :]
        o_ref[0, r:r + rt, :] = _gelu(y)


def _block_diag(x, nblk):
    g = x.shape[0] // nblk
    r, c = x.shape[1:]
    eye = jnp.eye(g, dtype=x.dtype)
    return jnp.einsum("jgrc,gh->jgrhc", x.reshape(nblk, g, r, c), eye).reshape(nblk, g * r, g * c)


def _s5(proj, col0, a_re, a_im, log_dt, b_re, b_im, c_re, c_im, dvec):
    b, s, _ = proj.shape
    g, n = a_re.shape
    width = g * S5_GROUP
    gpb = LANES // S5_GROUP
    nblk = g // gpb
    nst = gpb * n
    p_re, p_im, cr, ci = _s5_prep(a_re, a_im, log_dt)
    bmat_re = _block_diag(jnp.transpose(b_re, (0, 2, 1)), nblk)
    bmat_im = _block_diag(jnp.transpose(b_im, (0, 2, 1)), nblk)
    cmat_re = _block_diag(jnp.transpose(c_re, (0, 2, 1)), nblk)
    cmat_im = _block_diag(jnp.transpose(c_im, (0, 2, 1)), nblk)
    ublk = col0 // LANES
    st_row = lambda rows: pl.BlockSpec((rows, nst), lambda i, j: (0, j))
    return pl.pallas_call(
        _s5_body,
        grid=(b, nblk),
        in_specs=[pl.BlockSpec((1, s, LANES), lambda i, j: (i, 0, ublk + j)),
                  pl.BlockSpec((1, LANES, nst), lambda i, j: (j, 0, 0)),
                  pl.BlockSpec((1, LANES, nst), lambda i, j: (j, 0, 0)),
                  pl.BlockSpec((1, nst, LANES), lambda i, j: (j, 0, 0)),
                  pl.BlockSpec((1, nst, LANES), lambda i, j: (j, 0, 0)),
                  st_row(SCAN_ROWS), st_row(SCAN_ROWS), st_row(1), st_row(1),
                  pl.BlockSpec((1, LANES), lambda i, j: (0, j))],
        out_specs=pl.BlockSpec((1, s, LANES), lambda i, j: (i, 0, j)),
        out_shape=jax.ShapeDtypeStruct((b, s, width), F32),
        scratch_shapes=[pltpu.VMEM((s, nst), F32), pltpu.VMEM((s, nst), F32)],
        compiler_params=_params(("parallel", "parallel")),
    )(proj, bmat_re, bmat_im, cmat_re, cmat_im, p_re.reshape(SCAN_ROWS, g * n), p_im.reshape(SCAN_ROWS, g * n),
      cr.reshape(1, g * n), ci.reshape(1, g * n), dvec.reshape(1, width))


def _glu_body(y_ref, w_ref, b_ref, o_ref):
    y = y_ref[...]
    o_ref[...] = (y * _sigmoid(_dot(y.astype(BF16), w_ref[...]) + b_ref[...])).astype(o_ref.dtype)


def _glu(y, w, bias, tm=512):
    t, d = y.shape
    tm = min(tm, t)
    return pl.pallas_call(
        _glu_body,
        grid=(t // tm,),
        in_specs=[pl.BlockSpec((tm, d), lambda i: (i, 0)), pl.BlockSpec((d, d), lambda i: (0, 0)),
                  pl.BlockSpec((1, d), lambda i: (0, 0))],
        out_specs=pl.BlockSpec((tm, d), lambda i: (i, 0)),
        out_shape=jax.ShapeDtypeStruct((t, d), BF16),
        compiler_params=_params(("parallel",)),
    )(y, w.astype(BF16), bias.reshape(1, d))


PEER_RANK_NONE = 127.0


def _top_rows(s, k, with_rank=False):
    rows = []
    rank = jnp.full(s.shape, PEER_RANK_NONE, F32) if with_rank else None
    for i in range(k):
        m = jnp.max(s, axis=0, keepdims=True)
        rows.append(m)
        hit = s == m
        if with_rank:
            rank = jnp.where(hit, float(i), rank)
        s = jnp.where(hit, NEG_BIG, s)
    return (rows, rank) if with_rank else rows


def _peer_route_lanes(s1, s2):
    nk = PEER_TOPK + 1
    v1 = _top_rows(s1, nk)
    v2_rows, rank2 = _top_rows(s2, nk, with_rank=True)
    pad = jnp.full((-nk % SUBLANES, s1.shape[1]), NEG_BIG, F32)
    v2 = jnp.concatenate(v2_rows + [pad], axis=0)
    cand = [v1[0] + v2]
    for i in range(1, nk):
        need = nk // (i + 1)
        cand.append(v1[i] + v2[:-(-need // SUBLANES) * SUBLANES, :])
    top = _top_rows(jnp.concatenate(cand, axis=0), nk)
    zsum = jnp.ones_like(top[0])
    for r in top[1:PEER_TOPK]:
        zsum = zsum + jnp.exp(r - top[0])
    d1 = 0.5 * (top[PEER_TOPK - 1] + top[PEER_TOPK]) - s1
    cnt = jnp.zeros_like(s1)
    for r in v2_rows:
        cnt = cnt + jnp.where(r >= d1, 1.0, 0.0)
    e1 = jnp.exp(s1 - v1[0])
    e2 = jnp.exp(s2 - v2[0:1, :]) * (0.5 / zsum)
    return cnt, e1, rank2.astype(BF16), e2.astype(BF16)


def _peer_route_body(q_ref, keys_ref, u_ref, v_ref, cnt1_ref, e1_ref, rank2_ref, e2_ref, u16_ref, v16_ref):
    u16_ref[...] = u_ref[0].astype(BF16)
    v16_ref[...] = v_ref[0].astype(BF16)
    half = keys_ref.shape[3]
    q = q_ref[...]
    s1 = _dot_nt(keys_ref[0, 0], q[:, :half], HIGHEST)
    s2 = _dot_nt(keys_ref[0, 1], q[:, half:], HIGHEST)
    for c in range(0, q.shape[0], LANES):
        tok = slice(c, c + LANES)
        cnt, e1, rank2, e2 = _peer_route_lanes(s1[:, tok], s2[:, tok])
        cnt1_ref[0, :, tok] = cnt
        e1_ref[0, :, tok] = e1
        rank2_ref[0, :, tok] = rank2
        e2_ref[0, :, tok] = e2


def _peer_route(q, keys, u_all, v_all, layer, tt=512):
    t = q.shape[0]
    heads, _, nkeys, half = keys.shape
    _, e, d = u_all.shape
    tt = min(tt, t)
    steps = (t // tt) * heads
    er = e // steps
    assert e % steps == 0 and er % BF16_ROWS == 0
    table = pl.BlockSpec((1, nkeys, tt), lambda i, h: (h, 0, i))
    f32_shape = jax.ShapeDtypeStruct((heads, nkeys, t), F32)
    b16_shape = jax.ShapeDtypeStruct((heads, nkeys, t), BF16)
    rows_in = pl.BlockSpec((1, er, d), lambda i, h: (layer, i * heads + h, 0))
    rows_out = pl.BlockSpec((er, d), lambda i, h: (i * heads + h, 0))
    staged = jax.ShapeDtypeStruct((e, d), BF16)
    *tables, u16, v16 = pl.pallas_call(
        _peer_route_body,
        grid=(t // tt, heads),
        in_specs=[pl.BlockSpec((tt, 2 * half), lambda i, h: (i, h)),
                  pl.BlockSpec((1, 2, nkeys, half), lambda i, h: (h, 0, 0, 0)), rows_in, rows_in],
        out_specs=[table] * 4 + [rows_out, rows_out],
        out_shape=[f32_shape, f32_shape, b16_shape, b16_shape, staged, staged],
        compiler_params=_params(("parallel", "parallel")),
    )(q, keys, u_all, v_all)
    return tables, u16, v16


def _rows_bf16(row, n):
    one = jnp.broadcast_to(row, (BF16_ROWS, row.shape[1])).astype(BF16)
    return jnp.concatenate([one] * (n // BF16_ROWS), axis=0)


def _peer_dense_body(x_ref, u_ref, v_ref, cnt1_ref, e1_ref, rank2_ref, e2_ref, o_ref, acc_ref):
    heads, nkeys, _ = rank2_ref.shape
    te = u_ref.shape[0]
    j = pl.program_id(1)

    @pl.when(j == 0)
    def _():
        acc_ref[...] = jnp.zeros_like(acc_ref)

    pre = _dot_nt(u_ref[...], x_ref[...])
    act = pre * (1.0 + jnp.tanh(pre * (0.7978845608028654 + 0.035677408136300125 * (pre * pre))))
    parts = []
    for r in range(te // nkeys):
        i1 = j * (te // nkeys) + r
        w = None
        for h in range(heads):
            cnt = _rows_bf16(cnt1_ref[h, pl.ds(i1, 1), :], nkeys)
            e1 = _rows_bf16(e1_ref[h, pl.ds(i1, 1), :], nkeys)
            term = jnp.where(rank2_ref[h] < cnt, e2_ref[h] * e1, jnp.zeros((), BF16))
            w = term if w is None else w + term
        parts.append(w.astype(F32) * act[r * nkeys:(r + 1) * nkeys, :])
    p_t = jnp.concatenate(parts, axis=0) if len(parts) > 1 else parts[0]
    acc_ref[...] += _dot(p_t.T.astype(BF16), v_ref[...])

    @pl.when(j == pl.num_programs(1) - 1)
    def _():
        o_ref[...] = acc_ref[...].astype(o_ref.dtype)


def _peer_dense(x16, u16, v16, tables, tt=512, te=512):
    t, d = x16.shape
    e = u16.shape[0]
    heads, nkeys, _ = tables[0].shape
    tt = min(tt, t)
    te = min(te, e)
    assert te % nkeys == 0 and e == nkeys * nkeys
    table = pl.BlockSpec((heads, nkeys, tt), lambda i, j: (0, 0, i))
    return pl.pallas_call(
        _peer_dense_body,
        grid=(t // tt, e // te),
        in_specs=[pl.BlockSpec((tt, d), lambda i, j: (i, 0)),
                  pl.BlockSpec((te, d), lambda i, j: (j, 0)),
                  pl.BlockSpec((te, d), lambda i, j: (j, 0)),
                  table, table, table, table],
        out_specs=pl.BlockSpec((tt, d), lambda i, j: (i, 0)),
        out_shape=jax.ShapeDtypeStruct((t, d), BF16),
        scratch_shapes=[pltpu.VMEM((tt, d), F32)],
        compiler_params=_params(("parallel", "arbitrary")),
    )(x16, u16, v16, *tables)


def _peer_ffn(h16, wq16, keys, u_all, v_all, layer):
    q = _matmul(h16, wq16)
    tables, u16, v16 = _peer_route(q, keys, u_all, v_all, layer)
    return _peer_dense(h16, u16, v16, tables)


def _even_mixer(x2d, b, s, j, w_in_all, gdn_conv_w, a_log, dt_bias, norm_w, rg_conv_w, rg_conv_b,
                rg_wa, rg_ba, rg_wx, rg_bx, rg_lambda, w_out_all):
    heads = a_log.shape[0]
    gw = heads * HEAD_DIM
    rw = rg_lambda.shape[0]
    ab_pad = LANES - 2 * heads
    w_t = _stage_bf16(jnp.swapaxes(w_in_all, 1, 2), j)
    w_ab_t = jnp.pad(w_t[4 * gw:4 * gw + 2 * heads], ((0, ab_pad), (0, 0)))
    qkvz = _matmul_nt(x2d, w_t, n=4 * gw).reshape(b, s, 4 * gw)
    ab = _matmul_nt(x2d, w_ab_t).reshape(b, s, LANES)
    rg = _matmul_nt(x2d, w_t[4 * gw + 2 * heads:]).reshape(b, s, 2 * rw)
    gdn_out = _gdn(qkvz, ab, gdn_conv_w, a_log, dt_bias, norm_w, heads)
    rg_out = _rg_lru(rg, rg_conv_w, rg_conv_b, rg_wa, rg_wx, rg_ba, rg_bx, rg_lambda)
    return _matmul_pair(gdn_out.reshape(b * s, gw), rg_out.reshape(b * s, rw), _stage_bf16(w_out_all, j), out_dtype=BF16)


def _odd_mixer(x2d, b, s, j, w_in_all, a_re, a_im, log_dt, b_re, b_im, c_re, c_im, dvec, glu_w, glu_b,
               w_out_all):
    sw = dvec.shape[0]
    sbw = (w_in_all.shape[2] - sw) // 3
    heads = sbw // HEAD_DIM
    proj = _matmul(x2d, _stage_bf16(w_in_all, j)).reshape(b, s, 3 * sbw + sw)
    sb_out = _stick_breaking(proj, heads)
    yg = _s5(proj, 3 * sbw, a_re, a_im, log_dt, b_re, b_im, c_re, c_im, dvec)
    s5_out = _glu(yg.reshape(b * s, sw), glu_w, glu_b)
    return _matmul_pair(sb_out.reshape(b * s, sbw), s5_out, _stage_bf16(w_out_all, j), out_dtype=BF16)


def kernel(x, w_in_e, gdn_conv_w, gdn_A_log, gdn_dt_bias, gdn_norm_w, rg_conv_w, rg_conv_b, rg_wa, rg_ba,
           rg_wx, rg_bx, rg_lambda, w_out_e, w_in_o, s5_A_re, s5_A_im, s5_log_dt, s5_B_re, s5_B_im,
           s5_C_re, s5_C_im, s5_D, s5_glu_w, s5_glu_b, w_out_o, ln_mix_g, ln_mix_b, peer_wq, peer_keys,
           peer_u, peer_v, ln_ffn_g, ln_ffn_b):
    b, s, d = x.shape
    depth = ln_mix_g.shape[0]
    h = x.reshape(b * s, d)
    h16 = h.astype(BF16)
    for layer in range(depth):
        j = layer // 2
        if layer % 2 == 0:
            mix = _even_mixer(h16, b, s, j, w_in_e, gdn_conv_w[j], gdn_A_log[j], gdn_dt_bias[j], gdn_norm_w[j],
                              rg_conv_w[j], rg_conv_b[j], rg_wa[j], rg_ba[j], rg_wx[j], rg_bx[j],
                              rg_lambda[j], w_out_e)
        else:
            mix = _odd_mixer(h16, b, s, j, w_in_o, s5_A_re[j], s5_A_im[j], s5_log_dt[j], s5_B_re[j],
                             s5_B_im[j], s5_C_re[j], s5_C_im[j], s5_D[j], s5_glu_w[j], s5_glu_b[j], w_out_o)
        h, h16 = _residual_ln(h, mix, ln_mix_g[layer], ln_mix_b[layer])
        ffn = _peer_ffn(h16, _stage_bf16(peer_wq, layer), peer_keys[layer], peer_u, peer_v, layer)
        h, h16 = _residual_ln(h, ffn, ln_ffn_g[layer], ln_ffn_b[layer])
    return h.reshape(b, s, d)
```

```python
import functools
import math

import jax
import jax.numpy as jnp
from jax import lax
from jax.experimental import pallas as pl
from jax.experimental.pallas import tpu as pltpu

F32 = jnp.float32
BF16 = jnp.bfloat16
HIGHEST = lax.Precision.HIGHEST

LANES = 128
SUBLANES = 8
BF16_ROWS = 2 * SUBLANES
VMEM_BYTES_V7X = 64 * 1024 * 1024
VMEM_LIMIT = VMEM_BYTES_V7X - 8 * 1024 * 1024

HEAD_DIM = 128
CONV_W = 4
CONV_PAD = SUBLANES
GDN_CHUNK = 64
RG_BLOCK = 128
LRU_C = 8.0
S5_GROUP = 16
S5_STATE = 64
PEER_TOPK = 16
DEPTH = 2
DN_ALPHA = (2.0 * DEPTH) ** 0.25
LN_EPS = 1e-5
RMS_EPS = 1e-6
NEG_BIG = -3.0e38


def _params(sem, vmem=VMEM_LIMIT):
    return pltpu.CompilerParams(dimension_semantics=sem, vmem_limit_bytes=vmem)


def _softplus(x):
    return jnp.maximum(x, 0.0) + jnp.log1p(jnp.exp(-jnp.abs(x)))


def _sigmoid(x):
    return 1.0 / (1.0 + jnp.exp(-x))


def _gelu(x):
    return 0.5 * x * (1.0 + jnp.tanh(0.7978845608028654 * (x + 0.044715 * (x * x * x))))


def _dot(a, b, precision=None):
    return jnp.dot(a, b, preferred_element_type=F32, precision=precision)


def _split_bf16(x):
    hi = x.astype(BF16)
    return hi, (x - hi.astype(F32)).astype(BF16)


def _dot_split(a, b_hi, b_lo):
    a_hi, a_lo = _split_bf16(a)
    return _dot(a_hi, b_hi) + (_dot(a_lo, b_hi) + _dot(a_hi, b_lo))


def _dot_nt(a, b, precision=None):
    return lax.dot_general(a, b, (((1,), (1,)), ((), ())), preferred_element_type=F32, precision=precision)


def _mm_body(a_ref, b_ref, o_ref, acc_ref):
    k = pl.program_id(2)

    @pl.when(k == 0)
    def _():
        acc_ref[...] = jnp.zeros_like(acc_ref)

    acc_ref[...] += _dot(a_ref[...].astype(BF16), b_ref[...])

    @pl.when(k == pl.num_programs(2) - 1)
    def _():
        o_ref[...] = acc_ref[...].astype(o_ref.dtype)


def _mm_single_body(a_ref, b_ref, o_ref):
    o_ref[...] = _dot(a_ref[...].astype(BF16), b_ref[...]).astype(o_ref.dtype)


def _matmul(a, b, out_dtype=F32, tm=1024, tn=1024, tk=1024):
    m, kd = a.shape
    _, n = b.shape
    tm, tn = min(tm, m), min(tn, n)
    assert m % tm == 0 and n % tn == 0, (a.shape, b.shape)
    out_bytes = jnp.dtype(out_dtype).itemsize
    tile_bytes = 2 * (tm * kd * a.dtype.itemsize + kd * tn * b.dtype.itemsize + tm * tn * out_bytes)
    if tile_bytes <= (VMEM_LIMIT * 3) // 4:
        return pl.pallas_call(
            _mm_single_body,
            grid=(m // tm, n // tn),
            in_specs=[pl.BlockSpec((tm, kd), lambda i, j: (i, 0)),
                      pl.BlockSpec((kd, tn), lambda i, j: (0, j))],
            out_specs=pl.BlockSpec((tm, tn), lambda i, j: (i, j)),
            out_shape=jax.ShapeDtypeStruct((m, n), out_dtype),
            compiler_params=_params(("parallel", "parallel")),
        )(a, b)
    tk = min(tk, kd)
    assert kd % tk == 0, (a.shape, b.shape)
    return pl.pallas_call(
        _mm_body,
        grid=(m // tm, n // tn, kd // tk),
        in_specs=[pl.BlockSpec((tm, tk), lambda i, j, k: (i, k)),
                  pl.BlockSpec((tk, tn), lambda i, j, k: (k, j))],
        out_specs=pl.BlockSpec((tm, tn), lambda i, j, k: (i, j)),
        out_shape=jax.ShapeDtypeStruct((m, n), out_dtype),
        scratch_shapes=[pltpu.VMEM((tm, tn), F32)],
        compiler_params=_params(("parallel", "parallel", "arbitrary")),
    )(a, b)


def _mm_nt_body(a_ref, bt_ref, o_ref):
    o_ref[...] = _dot_nt(a_ref[...], bt_ref[...]).astype(o_ref.dtype)


def _matmul_nt(a, bt, n=None, out_dtype=F32, tm=1024, tn=1024):
    m, kd = a.shape
    n = bt.shape[0] if n is None else n
    tm, tn = min(tm, m), min(tn, n)
    assert m % tm == 0 and n % tn == 0 and bt.shape[1] == kd
    return pl.pallas_call(
        _mm_nt_body,
        grid=(m // tm, n // tn),
        in_specs=[pl.BlockSpec((tm, kd), lambda i, j: (i, 0)),
                  pl.BlockSpec((tn, kd), lambda i, j: (j, 0))],
        out_specs=pl.BlockSpec((tm, tn), lambda i, j: (i, j)),
        out_shape=jax.ShapeDtypeStruct((m, n), out_dtype),
        compiler_params=_params(("parallel", "parallel")),
    )(a, bt)


def _mm_pair_body(a1_ref, a2_ref, b1_ref, b2_ref, o_ref):
    o_ref[...] = (_dot(a1_ref[...], b1_ref[...]) + _dot(a2_ref[...], b2_ref[...])).astype(o_ref.dtype)


def _matmul_pair(a1, a2, b, out_dtype=F32, tm=1024, tn=1024):
    m, k1 = a1.shape
    _, k2 = a2.shape
    _, n = b.shape
    tm, tn = min(tm, m), min(tn, n)
    assert m % tm == 0 and n % tn == 0 and k1 % k2 == 0 and b.shape[0] == k1 + k2
    return pl.pallas_call(
        _mm_pair_body,
        grid=(m // tm, n // tn),
        in_specs=[pl.BlockSpec((tm, k1), lambda i, j: (i, 0)),
                  pl.BlockSpec((tm, k2), lambda i, j: (i, 0)),
                  pl.BlockSpec((k1, tn), lambda i, j: (0, j)),
                  pl.BlockSpec((k2, tn), lambda i, j: (k1 // k2, j))],
        out_specs=pl.BlockSpec((tm, tn), lambda i, j: (i, j)),
        out_shape=jax.ShapeDtypeStruct((m, n), out_dtype),
        compiler_params=_params(("parallel", "parallel")),
    )(a1, a2, b, b)


def _cast_body(x_ref, o_ref):
    o_ref[...] = x_ref[0].astype(o_ref.dtype)


def _stage_bf16(x, layer, tile_bytes=8 * 1024 * 1024):
    _, r, cols = x.shape
    assert cols % LANES == 0
    limit = tile_bytes // (cols * x.dtype.itemsize)
    tr = max([t for t in range(BF16_ROWS, min(r, limit) + 1, BF16_ROWS) if r % t == 0], default=r)
    return pl.pallas_call(
        _cast_body,
        grid=(r // tr,),
        in_specs=[pl.BlockSpec((1, tr, cols), lambda i: (layer, i, 0))],
        out_specs=pl.BlockSpec((tr, cols), lambda i: (i, 0)),
        out_shape=jax.ShapeDtypeStruct((r, cols), BF16),
        compiler_params=_params(("parallel",)),
    )(x)


def _ln_body(h_ref, d_ref, g_ref, b_ref, o_ref, ob_ref):
    y = DN_ALPHA * h_ref[...] + d_ref[...].astype(F32)
    mu = jnp.mean(y, axis=-1, keepdims=True)
    yc = y - mu
    var = jnp.mean(yc * yc, axis=-1, keepdims=True)
    out = yc * lax.rsqrt(var + LN_EPS) * g_ref[...] + b_ref[...]
    o_ref[...] = out
    ob_ref[...] = out.astype(BF16)


def _residual_ln(h, delta, g, b, tm=256):
    t, d = h.shape
    tm = min(tm, t)
    row = pl.BlockSpec((tm, d), lambda i: (i, 0))
    vec = pl.BlockSpec((1, d), lambda i: (0, 0))
    return pl.pallas_call(
        _ln_body,
        grid=(t // tm,),
        in_specs=[row, row, vec, vec],
        out_specs=[row, row],
        out_shape=[jax.ShapeDtypeStruct((t, d), F32), jax.ShapeDtypeStruct((t, d), BF16)],
        compiler_params=_params(("parallel",)),
    )(h, delta, g.reshape(1, d), b.reshape(1, d))


def _conv_rows(xpad, w, r, rows):
    acc = None
    for k in range(CONV_W):
        tap = xpad[pl.ds(r + CONV_PAD - (CONV_W - 1) + k, rows), :] * w[k:k + 1, :]
        acc = tap if acc is None else acc + tap
    return acc


def _row_tile(s):
    return min(s, 256)


def _bmm(a, b):
    return jnp.einsum("gmk,gkn->gmn", a.astype(BF16), b.astype(BF16), preferred_element_type=F32)


def _bmm_nt(a, b):
    return jnp.einsum("gmk,gnk->gmn", a.astype(BF16), b.astype(BF16), preferred_element_type=F32)


def _gdn_body(q_ref, k_ref, v_ref, z_ref, wq_ref, wk_ref, wv_ref, arow_ref, brow_ref,
              alog_ref, dtb_ref, nw_ref, o_ref, xq, xk, xv, state):
    ts = q_ref.shape[1]
    hb = q_ref.shape[2] // HEAD_DIM
    c = GDN_CHUNK
    nc = ts // c
    g = hb * nc

    @pl.when(pl.program_id(2) == 0)
    def _():
        for xp in (xq, xk, xv):
            xp[0:CONV_PAD, :] = jnp.zeros((CONV_PAD, hb * HEAD_DIM), F32)
        state[...] = jnp.zeros_like(state)

    def conv_act(x_ref, xp, w_ref, kind):
        xp[CONV_PAD:CONV_PAD + ts, :] = x_ref[0]
        w = w_ref[...]
        heads = []
        for h in range(hb):
            sl = slice(h * HEAD_DIM, (h + 1) * HEAD_DIM)
            y = None
            for k in range(CONV_W):
                tap = xp[pl.ds(CONV_PAD - (CONV_W - 1) + k, ts), sl] * w[k:k + 1, sl]
                y = tap if y is None else y + tap
            y = y * _sigmoid(y)
            if kind != "v":
                y = y * lax.rsqrt(jnp.sum(y * y, axis=-1, keepdims=True) + RMS_EPS)
            if kind == "q":
                y = y * (HEAD_DIM ** -0.5)
            heads.append(y.reshape(nc, c, HEAD_DIM))
        xp[0:CONV_PAD, :] = xp[ts:ts + CONV_PAD, :]
        return jnp.concatenate(heads, axis=0)

    qc = conv_act(q_ref, xq, wq_ref, "q")
    kc = conv_act(k_ref, xk, wk_ref, "k")
    vc = conv_act(v_ref, xv, wv_ref, "v")

    neg_a = -jnp.exp(alog_ref[...])[:, None]
    dtb = dtb_ref[...][:, None]
    grow = (neg_a * _softplus(arow_ref[0] + dtb)).reshape(g, 1, c)
    brow = _sigmoid(brow_ref[0]).reshape(g, 1, c)

    ri = lax.broadcasted_iota(jnp.int32, (c, c), 0)
    ci = lax.broadcasted_iota(jnp.int32, (c, c), 1)
    lower = ri >= ci
    diag = ri == ci
    gcol = jnp.sum(jnp.where(diag, grow, 0.0), axis=2, keepdims=True)
    beta = jnp.sum(jnp.where(diag, brow, 0.0), axis=2, keepdims=True)
    gc_col = jnp.sum(jnp.where(lower, grow, 0.0), axis=2, keepdims=True)
    gc_row = jnp.sum(jnp.where(ri <= ci, gcol, 0.0), axis=1, keepdims=True)
    gamma = jnp.where(lower, jnp.exp(jnp.where(lower, gc_col - gc_row, 0.0)), 0.0)
    kb = kc * beta
    p = jnp.where(ri > ci, -(_bmm_nt(kb, kc) * gamma), 0.0)
    r = p
    for _ in range(int(math.log2(c)) - 1):
        p = _bmm(p, p)
        r = r + p + _bmm(r, p)
    eg = jnp.exp(gc_col)
    vb = vc * beta
    kbe = kb * eg
    u = vb + _bmm(r, vb)
    w = kbe + _bmm(r, kbe)
    a_qk = _bmm_nt(qc, kc) * gamma
    g_last = gc_row[:, :, c - 1:c]
    k_tail = kc * jnp.exp(g_last - gc_col)
    q_dec = qc * eg
    decay = jnp.exp(g_last)

    per_head = lambda x: x.reshape((hb, nc) + x.shape[1:])
    u, w, a_qk, k_tail, q_dec, decay = [per_head(x) for x in (u, w, a_qk, k_tail, q_dec, decay)]
    nw = nw_ref[...]
    st = state[...]
    for i in range(nc):
        v_new = u[:, i] - _bmm(w[:, i], st)
        o = _bmm(q_dec[:, i], st) + _bmm(a_qk[:, i], v_new)
        kt = jnp.swapaxes(k_tail[:, i], 1, 2)
        st = st * decay[:, i] + _bmm(kt, v_new)
        o = o * lax.rsqrt(jnp.mean(o * o, axis=-1, keepdims=True) + RMS_EPS) * nw
        for h in range(hb):
            rows = slice(i * c, (i + 1) * c)
            cols = slice(h * HEAD_DIM, (h + 1) * HEAD_DIM)
            zz = z_ref[0, rows, cols]
            o_ref[0, rows, cols] = (o[h] * (zz * _sigmoid(zz))).astype(o_ref.dtype)
    state[...] = st


def _gdn(qkvz, ab, conv_w, a_log, dt_bias, norm_w, heads, heads_per_step=8, ts=256):
    b, s, _ = qkvz.shape
    c = GDN_CHUNK
    hb = min(heads_per_step, heads)
    ts = min(ts, s)
    ng = heads // hb
    rows = lambda x: jnp.transpose(x, (0, 2, 1)).reshape(b, heads, s // c, 1, c)
    arow = rows(ab[:, :, :heads])
    brow = rows(ab[:, :, heads:2 * heads])
    wide = hb * HEAD_DIM
    head = lambda off: pl.BlockSpec((1, ts, wide), lambda i, j, t: (i, t, off + j))
    cw = lambda off: pl.BlockSpec((CONV_W, wide), lambda i, j, t: (0, off + j))
    row = pl.BlockSpec((1, hb, ts // c, 1, c), lambda i, j, t: (i, j, t, 0, 0))
    scal = pl.BlockSpec((hb, 1, 1), lambda i, j, t: (j, 0, 0))
    return pl.pallas_call(
        _gdn_body,
        grid=(b, ng, s // ts),
        in_specs=[head(0), head(ng), head(2 * ng), head(3 * ng),
                  cw(0), cw(ng), cw(2 * ng), row, row,
                  scal, scal, pl.BlockSpec((1, HEAD_DIM), lambda i, j, t: (0, 0))],
        out_specs=pl.BlockSpec((1, ts, wide), lambda i, j, t: (i, t, j)),
        out_shape=jax.ShapeDtypeStruct((b, s, heads * HEAD_DIM), BF16),
        scratch_shapes=[pltpu.VMEM((ts + CONV_PAD, wide), F32)] * 3 + [pltpu.VMEM((hb, HEAD_DIM, HEAD_DIM), F32)],
        compiler_params=_params(("parallel", "parallel", "arbitrary")),
    )(qkvz, qkvz, qkvz, qkvz, conv_w, conv_w, conv_w, arow, brow,
      a_log.reshape(heads, 1, 1), dt_bias.reshape(heads, 1, 1), norm_w.reshape(1, HEAD_DIM))


SCAN_ROWS = SUBLANES
SCAN_SHIFTS = tuple(1 << i for i in range(int(math.log2(SCAN_ROWS))))


def _shift_rows(x, d, fill):
    row = lax.broadcasted_iota(jnp.int32, x.shape, 0)
    return jnp.where(row >= d, pltpu.roll(x, d, 0), fill)


def _rg_body(x_ref, gate_ref, cw_ref, cb_ref, wa_ref, wx_ref, ba_ref, bx_ref, lam_ref, o_ref,
             xpad, a_s, h_s):
    s = x_ref.shape[1]
    cb = x_ref.shape[2]
    nb = cb // RG_BLOCK
    rt = _row_tile(s)
    xpad[0:CONV_PAD, :] = jnp.zeros((CONV_PAD, cb), F32)
    xpad[CONV_PAD:CONV_PAD + s, :] = x_ref[0]
    w = cw_ref[...]
    bias = cb_ref[...]
    lam_sp = _softplus(-lam_ref[...])
    for r in range(0, s, rt):
        xr = _conv_rows(xpad, w, r, rt) + bias
        for n in range(nb):
            sl = slice(n * RG_BLOCK, (n + 1) * RG_BLOCK)
            xb = xr[:, sl]
            xb16 = xb.astype(BF16)
            rgate = _sigmoid(_dot(xb16, wa_ref[n]) + ba_ref[:, sl])
            igate = _sigmoid(_dot(xb16, wx_ref[n]) + bx_ref[:, sl])
            log_a = -LRU_C * rgate * lam_sp[:, sl]
            a = jnp.exp(log_a)
            a_s[r:r + rt, sl] = a
            one_minus_a2 = jnp.tanh(-log_a) * (a * a + 1.0)
            h_s[r:r + rt, sl] = jnp.sqrt(jnp.maximum(one_minus_a2, 0.0)) * (igate * xb)

    def step(i, carry):
        r0 = pl.multiple_of(i * SCAN_ROWS, SCAN_ROWS)
        a = a_s[pl.ds(r0, SCAN_ROWS), :]
        bv = h_s[pl.ds(r0, SCAN_ROWS), :]
        for d in SCAN_SHIFTS:
            bv = a * _shift_rows(bv, d, 0.0) + bv
            a = a * _shift_rows(a, d, 1.0)
        h = bv + a * carry
        h_s[pl.ds(r0, SCAN_ROWS), :] = h
        return h[SCAN_ROWS - 1:SCAN_ROWS, :]

    lax.fori_loop(0, s // SCAN_ROWS, step, jnp.zeros((1, cb), F32))
    for r in range(0, s, rt):
        o_ref[0, r:r + rt, :] = (h_s[r:r + rt, :] * _gelu(gate_ref[0, r:r + rt, :])).astype(o_ref.dtype)


def _rg_lru(rg, conv_w, conv_b, wa, wx, ba, bx, lam, blocks_per_step=4):
    b, s, w2 = rg.shape
    width = w2 // 2
    nblk = width // RG_BLOCK
    nb = min(blocks_per_step, nblk)
    cb = nb * RG_BLOCK
    steps = width // cb
    vec = pl.BlockSpec((1, cb), lambda i, j: (0, j))
    wblk = pl.BlockSpec((nb, RG_BLOCK, RG_BLOCK), lambda i, j: (j, 0, 0))
    return pl.pallas_call(
        _rg_body,
        grid=(b, steps),
        in_specs=[pl.BlockSpec((1, s, cb), lambda i, j: (i, 0, j)),
                  pl.BlockSpec((1, s, cb), lambda i, j: (i, 0, steps + j)),
                  pl.BlockSpec((CONV_W, cb), lambda i, j: (0, j)), vec, wblk, wblk, vec, vec, vec],
        out_specs=pl.BlockSpec((1, s, cb), lambda i, j: (i, 0, j)),
        out_shape=jax.ShapeDtypeStruct((b, s, width), BF16),
        scratch_shapes=[pltpu.VMEM((s + CONV_PAD, cb), F32), pltpu.VMEM((s, cb), F32), pltpu.VMEM((s, cb), F32)],
        compiler_params=_params(("parallel", "parallel")),
    )(rg, rg, conv_w, conv_b.reshape(1, width), wa.astype(BF16), wx.astype(BF16),
      ba.reshape(1, width), bx.reshape(1, width), lam.reshape(1, width))


SB_LOG_WEIGHT_FLOOR = -104.0


def _sb_body(q_ref, k_ref, v_ref, o_ref, *, bk):
    bq = q_ref.shape[1]
    hb = q_ref.shape[2] // HEAD_DIM
    qi = pl.program_id(2)
    t_pos = qi * bq + lax.broadcasted_iota(jnp.int32, (bq, bk), 0)
    s_off = lax.broadcasted_iota(jnp.int32, (bq, bk), 1)
    tri = (lax.broadcasted_iota(jnp.int32, (bk, bk), 0) > lax.broadcasted_iota(jnp.int32, (bk, bk), 1)).astype(BF16)
    nkb = (qi + 1) * (bq // bk)
    cols = [slice(h * HEAD_DIM, (h + 1) * HEAD_DIM) for h in range(hb)]
    q16 = [(q_ref[0, :, sl] * (HEAD_DIM ** -0.5)).astype(BF16) for sl in cols]

    def cond(carry):
        i, alive = carry[0], carry[1]
        return jnp.logical_and(i < nkb, alive)

    def body(carry):
        i, _, accs, runs = carry
        r0 = pl.multiple_of((nkb - 1 - i) * bk, bk)
        valid = (s_off + r0) < t_pos
        new_accs, new_runs = [], []
        top = None
        for h in range(hb):
            kblk = k_ref[0, pl.ds(r0, bk), cols[h]].astype(BF16)
            vblk = v_ref[0, pl.ds(r0, bk), cols[h]].astype(BF16)
            z = _dot_nt(q16[h], kblk)
            sp = _softplus(z)
            lm = jnp.where(valid, -sp, 0.0)
            hi = lm.astype(BF16)
            lo = (lm - hi.astype(F32)).astype(BF16)
            tail = _dot(hi, tri) + _dot(lo, tri) + runs[h]
            wts = jnp.where(valid, jnp.exp(z - sp + tail), 0.0)
            new_accs.append(accs[h] + _dot(wts.astype(BF16), vblk))
            run = runs[h] + jnp.sum(lm, axis=1, keepdims=True)
            new_runs.append(run)
            m = jnp.max(run)
            top = m if top is None else jnp.maximum(top, m)
        return i + 1, top > SB_LOG_WEIGHT_FLOOR, tuple(new_accs), tuple(new_runs)

    init = (jnp.int32(0), jnp.bool_(True),
            tuple(jnp.zeros((bq, HEAD_DIM), F32) for _ in range(hb)),
            tuple(jnp.zeros((bq, 1), F32) for _ in range(hb)))
    accs = lax.while_loop(cond, body, init)[2]
    for h in range(hb):
        o_ref[0, :, cols[h]] = accs[h].astype(o_ref.dtype)


def _stick_breaking(proj, heads, bq=256, bk=256, heads_per_step=2):
    b, s, _ = proj.shape
    bq = min(bq, s)
    bk = min(bk, bq)
    hb = min(heads_per_step, heads)
    ng = heads // hb
    wide = hb * HEAD_DIM
    return pl.pallas_call(
        functools.partial(_sb_body, bk=bk),
        grid=(b, ng, s // bq),
        in_specs=[pl.BlockSpec((1, bq, wide), lambda i, j, t: (i, t, j)),
                  pl.BlockSpec((1, s, wide), lambda i, j, t: (i, 0, ng + j)),
                  pl.BlockSpec((1, s, wide), lambda i, j, t: (i, 0, 2 * ng + j))],
        out_specs=pl.BlockSpec((1, bq, wide), lambda i, j, t: (i, t, j)),
        out_shape=jax.ShapeDtypeStruct((b, s, heads * HEAD_DIM), BF16),
        compiler_params=_params(("parallel", "parallel", "arbitrary")),
    )(proj, proj, proj)


def _s5_prep_body(are_ref, aim_ref, ldt_ref, abre_ref, abim_ref, cr_ref, ci_ref):
    dt = jnp.exp(ldt_ref[...])
    lr = jnp.minimum(are_ref[...], -1e-4)
    li = aim_ref[...]
    mag = jnp.exp(lr * dt)
    ab_re = mag * jnp.cos(li * dt)
    ab_im = mag * jnp.sin(li * dt)
    den = lr * lr + li * li
    nr = ab_re - 1.0
    cr_ref[...] = (nr * lr + ab_im * li) / den
    ci_ref[...] = (ab_im * lr - nr * li) / den
    p_re, p_im = ab_re, ab_im
    for k in range(SCAN_ROWS):
        abre_ref[k] = p_re
        abim_ref[k] = p_im
        p_re, p_im = p_re * ab_re - p_im * ab_im, p_re * ab_im + p_im * ab_re


def _s5_prep(a_re, a_im, log_dt):
    g, n = a_re.shape
    full = pl.BlockSpec((g, n), lambda: (0, 0))
    pw = pl.BlockSpec((SCAN_ROWS, g, n), lambda: (0, 0, 0))
    return pl.pallas_call(
        _s5_prep_body,
        in_specs=[full, full, pl.BlockSpec((g, 1), lambda: (0, 0))],
        out_specs=[pw, pw, full, full],
        out_shape=[jax.ShapeDtypeStruct((SCAN_ROWS, g, n), F32)] * 2 + [jax.ShapeDtypeStruct((g, n), F32)] * 2,
    )(a_re, a_im, log_dt.reshape(g, 1))


def _s5_body(u_ref, bre_ref, bim_ref, cre_ref, cim_ref, pre_ref, pim_ref, cr_ref, ci_ref, d_ref, o_ref,
             hre_s, him_s):
    s = u_ref.shape[1]
    rt = _row_tile(s)
    cr = cr_ref[...]
    ci = ci_ref[...]
    b_re = _split_bf16(bre_ref[0])
    b_im = _split_bf16(bim_ref[0])
    for r in range(0, s, rt):
        u = u_ref[0, r:r + rt, :]
        xr = _dot_split(u, *b_re)
        xi = _dot_split(u, *b_im)
        hre_s[r:r + rt, :] = cr * xr - ci * xi
        him_s[r:r + rt, :] = cr * xi + ci * xr
    p_re = pre_ref[...]
    p_im = pim_ref[...]

    def step(i, carry):
        c_re, c_im = carry
        r0 = pl.multiple_of(i * SCAN_ROWS, SCAN_ROWS)
        x_re = hre_s[pl.ds(r0, SCAN_ROWS), :]
        x_im = him_s[pl.ds(r0, SCAN_ROWS), :]
        for d in SCAN_SHIFTS:
            a_re = p_re[d - 1:d, :]
            a_im = p_im[d - 1:d, :]
            s_re = _shift_rows(x_re, d, 0.0)
            s_im = _shift_rows(x_im, d, 0.0)
            x_re, x_im = x_re + a_re * s_re - a_im * s_im, x_im + a_re * s_im + a_im * s_re
        h_re = x_re + p_re * c_re - p_im * c_im
        h_im = x_im + p_re * c_im + p_im * c_re
        hre_s[pl.ds(r0, SCAN_ROWS), :] = h_re
        him_s[pl.ds(r0, SCAN_ROWS), :] = h_im
        return h_re[SCAN_ROWS - 1:SCAN_ROWS, :], h_im[SCAN_ROWS - 1:SCAN_ROWS, :]

    zero = jnp.zeros((1, hre_s.shape[1]), F32)
    lax.fori_loop(0, s // SCAN_ROWS, step, (zero, zero))
    dvec = d_ref[...]
    cmat_re = _split_bf16(cre_ref[0])
    cmat_im = _split_bf16(cim_ref[0])
    for r in range(0, s, rt):
        y = _dot_split(hre_s[r:r + rt, :], *cmat_re) - _dot_split(him_s[r:r + rt, :], *cmat_im)
        y = y + dvec * u_ref[0, r:r + rt, :]
        o_ref[0, r:r + rt, :] = _gelu(y)


def _block_diag(x, nblk):
    g = x.shape[0] // nblk
    r, c = x.shape[1:]
    eye = jnp.eye(g, dtype=x.dtype)
    return jnp.einsum("jgrc,gh->jgrhc", x.reshape(nblk, g, r, c), eye).reshape(nblk, g * r, g * c)


def _s5(proj, col0, a_re, a_im, log_dt, b_re, b_im, c_re, c_im, dvec):
    b, s, _ = proj.shape
    g, n = a_re.shape
    width = g * S5_GROUP
    gpb = LANES // S5_GROUP
    nblk = g // gpb
    nst = gpb * n
    p_re, p_im, cr, ci = _s5_prep(a_re, a_im, log_dt)
    bmat_re = _block_diag(jnp.transpose(b_re, (0, 2, 1)), nblk)
    bmat_im = _block_diag(jnp.transpose(b_im, (0, 2, 1)), nblk)
    cmat_re = _block_diag(jnp.transpose(c_re, (0, 2, 1)), nblk)
    cmat_im = _block_diag(jnp.transpose(c_im, (0, 2, 1)), nblk)
    ublk = col0 // LANES
    st_row = lambda rows: pl.BlockSpec((rows, nst), lambda i, j: (0, j))
    return pl.pallas_call(
        _s5_body,
        grid=(b, nblk),
        in_specs=[pl.BlockSpec((1, s, LANES), lambda i, j: (i, 0, ublk + j)),
                  pl.BlockSpec((1, LANES, nst), lambda i, j: (j, 0, 0)),
                  pl.BlockSpec((1, LANES, nst), lambda i, j: (j, 0, 0)),
                  pl.BlockSpec((1, nst, LANES), lambda i, j: (j, 0, 0)),
                  pl.BlockSpec((1, nst, LANES), lambda i, j: (j, 0, 0)),
                  st_row(SCAN_ROWS), st_row(SCAN_ROWS), st_row(1), st_row(1),
                  pl.BlockSpec((1, LANES), lambda i, j: (0, j))],
        out_specs=pl.BlockSpec((1, s, LANES), lambda i, j: (i, 0, j)),
        out_shape=jax.ShapeDtypeStruct((b, s, width), F32),
        scratch_shapes=[pltpu.VMEM((s, nst), F32), pltpu.VMEM((s, nst), F32)],
        compiler_params=_params(("parallel", "parallel")),
    )(proj, bmat_re, bmat_im, cmat_re, cmat_im, p_re.reshape(SCAN_ROWS, g * n), p_im.reshape(SCAN_ROWS, g * n),
      cr.reshape(1, g * n), ci.reshape(1, g * n), dvec.reshape(1, width))


def _glu_body(y_ref, w_ref, b_ref, o_ref):
    y = y_ref[...]
    o_ref[...] = (y * _sigmoid(_dot(y.astype(BF16), w_ref[...]) + b_ref[...])).astype(o_ref.dtype)


def _glu(y, w, bias, tm=512):
    t, d = y.shape
    tm = min(tm, t)
    return pl.pallas_call(
        _glu_body,
        grid=(t // tm,),
        in_specs=[pl.BlockSpec((tm, d), lambda i: (i, 0)), pl.BlockSpec((d, d), lambda i: (0, 0)),
                  pl.BlockSpec((1, d), lambda i: (0, 0))],
        out_specs=pl.BlockSpec((tm, d), lambda i: (i, 0)),
        out_shape=jax.ShapeDtypeStruct((t, d), BF16),
        compiler_params=_params(("parallel",)),
    )(y, w.astype(BF16), bias.reshape(1, d))


PEER_RANK_NONE = 127.0


def _top_rows(s, k, with_rank=False):
    rows = []
    rank = jnp.full(s.shape, PEER_RANK_NONE, F32) if with_rank else None
    for i in range(k):
        m = jnp.max(s, axis=0, keepdims=True)
        rows.append(m)
        hit = s == m
        if with_rank:
            rank = jnp.where(hit, float(i), rank)
        s = jnp.where(hit, NEG_BIG, s)
    return (rows, rank) if with_rank else rows


def _peer_route_lanes(s1, s2):
    nk = PEER_TOPK + 1
    v1 = _top_rows(s1, nk)
    v2_rows, rank2 = _top_rows(s2, nk, with_rank=True)
    pad = jnp.full((-nk % SUBLANES, s1.shape[1]), NEG_BIG, F32)
    v2 = jnp.concatenate(v2_rows + [pad], axis=0)
    cand = [v1[0] + v2]
    for i in range(1, nk):
        need = nk // (i + 1)
        cand.append(v1[i] + v2[:-(-need // SUBLANES) * SUBLANES, :])
    cand = jnp.concatenate(cand, axis=0)
    top = _top_rows(cand, nk)
    thr = 0.5 * (top[PEER_TOPK - 1] + top[PEER_TOPK])
    zsum = jnp.ones_like(top[0])
    for r in top[1:PEER_TOPK]:
        zsum = zsum + jnp.exp(r - top[0])
    d1 = thr - s1
    cnt = jnp.zeros_like(s1)
    for r in v2_rows:
        cnt = cnt + jnp.where(r >= d1, 1.0, 0.0)
    e1 = jnp.exp(s1 - v1[0])
    e2 = jnp.exp(s2 - v2[0:1, :]) * (0.5 / zsum)
    at_least = lambda x, v: jnp.sum(jnp.where(x >= v, 1.0, 0.0), axis=0, keepdims=True)
    ok = jnp.logical_and(jnp.logical_and(at_least(s1, v1[nk - 1]) == nk, at_least(s2, v2_rows[nk - 1]) == nk),
                         at_least(cand, thr) == PEER_TOPK)
    return (cnt, e1, rank2.astype(BF16), e2.astype(BF16)), jnp.min(jnp.where(ok, 1.0, 0.0)) > 0.5


def _top_strict(s, k):
    idx = lax.broadcasted_iota(jnp.int32, s.shape, 0).astype(F32)
    rank = jnp.full(s.shape, PEER_RANK_NONE, F32)
    rows = []
    for i in range(k):
        m = jnp.max(s, axis=0, keepdims=True)
        hit = s == m
        first = jnp.min(jnp.where(hit, idx, float(s.shape[0])), axis=0, keepdims=True)
        chosen = jnp.logical_and(hit, idx == first)
        rows.append(m)
        rank = jnp.where(chosen, float(i), rank)
        s = jnp.where(chosen, NEG_BIG, s)
    return rows, rank, jnp.where(rank < float(k), 1.0, 0.0)


def _peer_route_lanes_exact(s1, s2):
    k = PEER_TOPK
    v1, rank1, _ = _top_strict(s1, k)
    v2, rank2, _ = _top_strict(s2, k)
    v2 = jnp.concatenate(v2, axis=0)
    cand = jnp.concatenate([r + v2 for r in v1], axis=0)
    top, _, chosen = _top_strict(cand, k)
    zsum = jnp.sum(chosen * jnp.exp(cand - top[0]), axis=0, keepdims=True)
    cnt = jnp.zeros_like(s1)
    for a in range(k):
        cnt_a = jnp.sum(chosen[a * k:(a + 1) * k, :], axis=0, keepdims=True)
        cnt = cnt + jnp.where(rank1 == float(a), cnt_a, 0.0)
    e1 = jnp.exp(s1 - v1[0])
    e2 = jnp.exp(s2 - v2[0:1, :]) * (0.5 / zsum)
    return cnt, e1, rank2.astype(BF16), e2.astype(BF16)


def _peer_route_body(q_ref, keys_ref, *refs, n_stage):
    src_refs, refs = refs[:n_stage], refs[n_stage:]
    (cnt1_ref, e1_ref, rank2_ref, e2_ref), dst_refs = refs[:4], refs[4:]
    for src, dst in zip(src_refs, dst_refs):
        dst[...] = src[0].astype(BF16)
    half = keys_ref.shape[3]
    q = q_ref[...]
    s1 = _dot_nt(keys_ref[0, 0], q[:, :half], HIGHEST)
    s2 = _dot_nt(keys_ref[0, 1], q[:, half:], HIGHEST)
    def store(tok, tables):
        for ref, val in zip((cnt1_ref, e1_ref, rank2_ref, e2_ref), tables):
            ref[0, :, tok] = val

    toks = [slice(c, c + LANES) for c in range(0, q.shape[0], LANES)]
    distinct = []
    for tok in toks:
        tables, ok = _peer_route_lanes(s1[:, tok], s2[:, tok])
        store(tok, tables)
        distinct.append(ok)
    for tok, ok in zip(toks, distinct):
        @pl.when(jnp.logical_not(ok))
        def _():
            store(tok, _peer_route_lanes_exact(s1[:, tok], s2[:, tok]))


def _route_steps(t, heads, tt=512):
    return (t // min(tt, t)) * heads


def _can_ride(x, steps):
    return x.shape[1] % steps == 0 and (x.shape[1] // steps) % BF16_ROWS == 0 and x.shape[2] % LANES == 0


def _peer_route(q, keys, stage, tt=512):
    t = q.shape[0]
    heads, _, nkeys, half = keys.shape
    tt = min(tt, t)
    steps = _route_steps(t, heads, tt)
    table = pl.BlockSpec((1, nkeys, tt), lambda i, h: (h, 0, i))
    f32_shape = jax.ShapeDtypeStruct((heads, nkeys, t), F32)
    b16_shape = jax.ShapeDtypeStruct((heads, nkeys, t), BF16)
    rows_in, rows_out, staged = [], [], []
    for x, layer in stage:
        assert _can_ride(x, steps)
        _, r, c = x.shape
        rows_in.append(pl.BlockSpec((1, r // steps, c), lambda i, h, layer=layer: (layer, i * heads + h, 0)))
        rows_out.append(pl.BlockSpec((r // steps, c), lambda i, h: (i * heads + h, 0)))
        staged.append(jax.ShapeDtypeStruct((r, c), BF16))
    outs = pl.pallas_call(
        functools.partial(_peer_route_body, n_stage=len(stage)),
        grid=(t // tt, heads),
        in_specs=[pl.BlockSpec((tt, 2 * half), lambda i, h: (i, h)),
                  pl.BlockSpec((1, 2, nkeys, half), lambda i, h: (h, 0, 0, 0))] + rows_in,
        out_specs=[table] * 4 + rows_out,
        out_shape=[f32_shape, f32_shape, b16_shape, b16_shape] + staged,
        compiler_params=_params(("parallel", "parallel")),
    )(q, keys, *[x for x, _ in stage])
    return outs[:4], outs[4:]


def _rows_bf16(row, n):
    one = jnp.broadcast_to(row, (BF16_ROWS, row.shape[1])).astype(BF16)
    return jnp.concatenate([one] * (n // BF16_ROWS), axis=0)


def _peer_dense_body(x_ref, u_ref, v_ref, cnt1_ref, e1_ref, rank2_ref, e2_ref, o_ref, acc_ref):
    heads, nkeys, _ = rank2_ref.shape
    te = u_ref.shape[0]
    j = pl.program_id(1)

    @pl.when(j == 0)
    def _():
        acc_ref[...] = jnp.zeros_like(acc_ref)

    pre = _dot_nt(u_ref[...], x_ref[...])
    act = pre * (1.0 + jnp.tanh(pre * (0.7978845608028654 + 0.035677408136300125 * (pre * pre))))
    parts = []
    for r in range(te // nkeys):
        i1 = j * (te // nkeys) + r
        w = None
        for h in range(heads):
            cnt = _rows_bf16(cnt1_ref[h, pl.ds(i1, 1), :], nkeys)
            e1 = _rows_bf16(e1_ref[h, pl.ds(i1, 1), :], nkeys)
            term = jnp.where(rank2_ref[h] < cnt, e2_ref[h] * e1, jnp.zeros((), BF16))
            w = term if w is None else w + term
        parts.append(w.astype(F32) * act[r * nkeys:(r + 1) * nkeys, :])
    p_t = jnp.concatenate(parts, axis=0) if len(parts) > 1 else parts[0]
    acc_ref[...] += _dot(p_t.T.astype(BF16), v_ref[...])

    @pl.when(j == pl.num_programs(1) - 1)
    def _():
        o_ref[...] = acc_ref[...].astype(o_ref.dtype)


def _peer_dense(x16, u16, v16, tables, tt=512, te=512):
    t, d = x16.shape
    e = u16.shape[0]
    heads, nkeys, _ = tables[0].shape
    tt = min(tt, t)
    te = min(te, e)
    assert te % nkeys == 0 and e == nkeys * nkeys
    table = pl.BlockSpec((heads, nkeys, tt), lambda i, j: (0, 0, i))
    return pl.pallas_call(
        _peer_dense_body,
        grid=(t // tt, e // te),
        in_specs=[pl.BlockSpec((tt, d), lambda i, j: (i, 0)),
                  pl.BlockSpec((te, d), lambda i, j: (j, 0)),
                  pl.BlockSpec((te, d), lambda i, j: (j, 0)),
                  table, table, table, table],
        out_specs=pl.BlockSpec((tt, d), lambda i, j: (i, 0)),
        out_shape=jax.ShapeDtypeStruct((t, d), BF16),
        scratch_shapes=[pltpu.VMEM((tt, d), F32)],
        compiler_params=_params(("parallel", "arbitrary")),
    )(x16, u16, v16, *tables)


def _peer_ffn(h16, wq16, keys, u_all, v_all, layer, extra=()):
    q = _matmul(h16, wq16)
    tables, (u16, v16, *rest) = _peer_route(q, keys, [(u_all, layer), (v_all, layer), *extra])
    return _peer_dense(h16, u16, v16, tables), rest


def _even_mixer(x2d, b, s, j, w_in_all, gdn_conv_w, a_log, dt_bias, norm_w, rg_conv_w, rg_conv_b,
                rg_wa, rg_ba, rg_wx, rg_bx, rg_lambda, w_out_all):
    heads = a_log.shape[0]
    gw = heads * HEAD_DIM
    rw = rg_lambda.shape[0]
    ab_pad = LANES - 2 * heads
    w_t = _stage_bf16(jnp.swapaxes(w_in_all, 1, 2), j)
    w_ab_t = jnp.pad(w_t[4 * gw:4 * gw + 2 * heads], ((0, ab_pad), (0, 0)))
    qkvz = _matmul_nt(x2d, w_t, n=4 * gw).reshape(b, s, 4 * gw)
    ab = _matmul_nt(x2d, w_ab_t).reshape(b, s, LANES)
    rg = _matmul_nt(x2d, w_t[4 * gw + 2 * heads:]).reshape(b, s, 2 * rw)
    gdn_out = _gdn(qkvz, ab, gdn_conv_w, a_log, dt_bias, norm_w, heads)
    rg_out = _rg_lru(rg, rg_conv_w, rg_conv_b, rg_wa, rg_wx, rg_ba, rg_bx, rg_lambda)
    return _matmul_pair(gdn_out.reshape(b * s, gw), rg_out.reshape(b * s, rw), _stage_bf16(w_out_all, j), out_dtype=BF16)


def _odd_mixer(x2d, b, s, j, w_in_all, a_re, a_im, log_dt, b_re, b_im, c_re, c_im, dvec, glu_w, glu_b,
               w_out_all, w_in16=None):
    sw = dvec.shape[0]
    sbw = (w_in_all.shape[2] - sw) // 3
    heads = sbw // HEAD_DIM
    w_in16 = _stage_bf16(w_in_all, j) if w_in16 is None else w_in16
    proj = _matmul(x2d, w_in16).reshape(b, s, 3 * sbw + sw)
    sb_out = _stick_breaking(proj, heads)
    yg = _s5(proj, 3 * sbw, a_re, a_im, log_dt, b_re, b_im, c_re, c_im, dvec)
    s5_out = _glu(yg.reshape(b * s, sw), glu_w, glu_b)
    return _matmul_pair(sb_out.reshape(b * s, sbw), s5_out, _stage_bf16(w_out_all, j), out_dtype=BF16)


def kernel(x, w_in_e, gdn_conv_w, gdn_A_log, gdn_dt_bias, gdn_norm_w, rg_conv_w, rg_conv_b, rg_wa, rg_ba,
           rg_wx, rg_bx, rg_lambda, w_out_e, w_in_o, s5_A_re, s5_A_im, s5_log_dt, s5_B_re, s5_B_im,
           s5_C_re, s5_C_im, s5_D, s5_glu_w, s5_glu_b, w_out_o, ln_mix_g, ln_mix_b, peer_wq, peer_keys,
           peer_u, peer_v, ln_ffn_g, ln_ffn_b):
    b, s, d = x.shape
    depth = ln_mix_g.shape[0]
    h = x.reshape(b * s, d)
    h16 = h.astype(BF16)
    steps = _route_steps(b * s, peer_keys.shape[1])
    ready = {}
    for layer in range(depth):
        j = layer // 2
        if layer % 2 == 0:
            mix = _even_mixer(h16, b, s, j, w_in_e, gdn_conv_w[j], gdn_A_log[j], gdn_dt_bias[j], gdn_norm_w[j],
                              rg_conv_w[j], rg_conv_b[j], rg_wa[j], rg_ba[j], rg_wx[j], rg_bx[j],
                              rg_lambda[j], w_out_e)
        else:
            mix = _odd_mixer(h16, b, s, j, w_in_o, s5_A_re[j], s5_A_im[j], s5_log_dt[j], s5_B_re[j],
                             s5_B_im[j], s5_C_re[j], s5_C_im[j], s5_D[j], s5_glu_w[j], s5_glu_b[j], w_out_o,
                             w_in16=ready.pop("w_in", None))
        h, h16 = _residual_ln(h, mix, ln_mix_g[layer], ln_mix_b[layer])
        wq16 = ready.pop("wq") if "wq" in ready else _stage_bf16(peer_wq, layer)
        extra = {}
        if layer + 1 < depth and _can_ride(peer_wq, steps):
            extra["wq"] = (peer_wq, layer + 1)
        if layer + 1 < depth and (layer + 1) % 2 == 1 and _can_ride(w_in_o, steps):
            extra["w_in"] = (w_in_o, (layer + 1) // 2)
        ffn, staged = _peer_ffn(h16, wq16, peer_keys[layer], peer_u, peer_v, layer, tuple(extra.values()))
        ready = dict(zip(extra.keys(), staged))
        h, h16 = _residual_ln(h, ffn, ln_ffn_g[layer], ln_ffn_b[layer])
    return h.reshape(b, s, d)
```
